```python
import math
import numpy as np
import jax
import jax.numpy as jnp
from jax import lax

D_MODEL = 4096
BATCH = 1
SEQ = 8192
DEPTH = 4

GRID_W = 64
CTX_LEN = 256
CHUNK = 128
Q_BLOCK = 128
ROPE_BASE = 10000.0
EPS = 1e-6

RET_HEAD_DIM = 128
RET_WIDTH = 3 * D_MODEL // 8
RET_HEADS = RET_WIDTH // RET_HEAD_DIM
DIF_V_DIM = 128
DIF_QK_DIM = DIF_V_DIM // 2
DIF_WIDTH = 3 * D_MODEL // 8
DIF_HEADS = DIF_WIDTH // DIF_V_DIM
DIF_QK_WIDTH = DIF_HEADS * 2 * DIF_QK_DIM
MLP_WIDTH = D_MODEL - RET_WIDTH - DIF_WIDTH
MLP_GROUPS = 4
MLP_GROUP_DIM = MLP_WIDTH // MLP_GROUPS
MIX_WIDTH = RET_WIDTH + DIF_WIDTH + MLP_WIDTH

KV_SPLITS = (RET_WIDTH, RET_WIDTH, DIF_QK_WIDTH, DIF_WIDTH)
REST_SPLITS = (RET_WIDTH, DIF_QK_WIDTH, RET_WIDTH, DIF_WIDTH, MLP_WIDTH, MLP_WIDTH, MLP_WIDTH)
KV_COLS = sum(KV_SPLITS)
IN_COLS = KV_COLS + sum(REST_SPLITS)

kernel_name = "hybrid_retention_diffattn_chunkmlp_dit"


def rms_norm(x, w=None):
    xf = x.astype(jnp.float32)
    y = xf * lax.rsqrt(jnp.mean(xf * xf, axis=-1, keepdims=True) + EPS)
    if w is not None:
        y = y * w.astype(jnp.float32)
    return y.astype(x.dtype)


def split_cols(z, sizes):
    idx = np.cumsum(sizes)[:-1].tolist()
    return jnp.split(z, idx, axis=-1)


def modulation(cvec, w_ada, b_ada):
    m = jax.nn.silu(cvec) @ w_ada + b_ada
    return jnp.split(m, 3, axis=-1)


def axial_rope(n_tokens, dim):
    rows = n_tokens // GRID_W
    r, c = jnp.meshgrid(jnp.arange(rows, dtype=jnp.float32),
                        jnp.arange(GRID_W, dtype=jnp.float32), indexing="ij")
    n_freq = dim // 4
    inv = ROPE_BASE ** (-jnp.arange(n_freq, dtype=jnp.float32) / n_freq)
    ang = jnp.concatenate([r.reshape(-1, 1) * inv, c.reshape(-1, 1) * inv], axis=-1)
    return jnp.cos(ang), jnp.sin(ang)


def apply_rope(x, cos, sin):
    x1, x2 = jnp.split(x, 2, axis=-1)
    return jnp.concatenate([x1 * cos - x2 * sin, x2 * cos + x1 * sin], axis=-1).astype(x.dtype)


def heads_first(t, n_heads):
    b, n, _ = t.shape
    return t.reshape(b, n, n_heads, -1).transpose(0, 2, 1, 3)


def dif_heads(t):
    b, n, _ = t.shape
    return t.reshape(b, n, DIF_HEADS, 2, DIF_QK_DIM)


def retention_dir(q, k, v, log_gamma, s0, strict):
    b, h, n, dk = q.shape
    dv = v.shape[-1]
    nc = n // CHUNK
    qc = q.reshape(b, h, nc, CHUNK, dk)
    kc = k.reshape(b, h, nc, CHUNK, dk)
    vc = v.reshape(b, h, nc, CHUNK, dv)
    pos = jnp.arange(CHUNK, dtype=jnp.float32)
    lg = log_gamma[:, None]
    rel = pos[:, None] - pos[None, :]
    keep = rel > 0 if strict else rel >= 0
    decay = jnp.where(keep, jnp.exp(lg[:, :, None] * jnp.maximum(rel, 0.0)), 0.0)
    scores = jnp.einsum("bhcid,bhcjd->bhcij", qc, kc) * decay[None, :, None]
    o_inner = jnp.einsum("bhcij,bhcje->bhcie", scores, vc)
    k_decay = jnp.exp(lg * (CHUNK - 1 - pos))
    kv = jnp.einsum("bhcjd,hj,bhcje->cbhde", kc, k_decay, vc)
    chunk_decay = jnp.exp(log_gamma * CHUNK)[None, :, None, None]

    def step(s, kv_c):
        return s * chunk_decay + kv_c, s

    _, s_prev = lax.scan(step, s0, kv)
    q_decay = jnp.exp(lg * (pos + 1.0))
    o_cross = jnp.einsum("bhcid,hi,cbhde->bhcie", qc, q_decay, s_prev)
    return (o_inner + o_cross).reshape(b, h, n, dv)


def bi_retention(q, k, v, lg_f, lg_b, s0_f, s0_b):
    o_f = retention_dir(q, k, v, lg_f, s0_f, False)
    o_b = retention_dir(jnp.flip(q, 2), jnp.flip(k, 2), jnp.flip(v, 2), lg_b, s0_b, True)
    return o_f + jnp.flip(o_b, 2)


def context_states(k, v, lg_f, lg_b):
    m = k.shape[2]
    pos = jnp.arange(m, dtype=jnp.float32)
    w_f = jnp.exp(lg_f[:, None] * (m - 1 - pos))
    w_b = jnp.exp(lg_b[:, None] * pos)
    s_f = jnp.einsum("bhmd,hm,bhme->bhde", k, w_f, v)
    s_b = jnp.einsum("bhmd,hm,bhme->bhde", k, w_b, v)
    return s_f, s_b


def retention_out(o, gate):
    o = rms_norm(o)
    b, h, n, d = o.shape
    o = o.transpose(0, 2, 1, 3).reshape(b, n, h * d).astype(gate.dtype)
    return o * jax.nn.silu(gate)


def diff_attention(q, k, v, lam):
    b, n = q.shape[:2]
    qb = jnp.moveaxis(q.reshape(b, n // Q_BLOCK, Q_BLOCK, DIF_HEADS, 2, DIF_QK_DIM), 1, 0)

    def block(qi):
        s = jnp.einsum("bqhcd,bkhcd->bhcqk", qi, k).astype(jnp.float32)
        p = jax.nn.softmax(s, axis=-1)
        w = p[:, :, 0] - lam * p[:, :, 1]
        return jnp.einsum("bhqk,bkhe->bqhe", w.astype(v.dtype), v)

    o = lax.map(block, qb)
    return jnp.moveaxis(o, 0, 1).reshape(b, n, DIF_HEADS, DIF_V_DIM)


def diff_out(o, norm_w, lambda_init, gate):
    b, n = o.shape[:2]
    o = rms_norm(o, norm_w) * (1.0 - lambda_init)
    return o.reshape(b, n, DIF_WIDTH).astype(gate.dtype) * jax.nn.silu(gate)


def chunk_mlp(u, v, ws, bs, vnorm_w, gate):
    v = rms_norm(v, vnorm_w)
    b, n, _ = v.shape
    vg = v.reshape(b, n // CHUNK, CHUNK, MLP_GROUPS, MLP_GROUP_DIM)
    mixed = jnp.einsum("gij,bcjgd->bcigd", ws, vg) + bs.T[None, None, :, :, None]
    return u * mixed.reshape(b, n, MLP_WIDTH) * jax.nn.silu(gate)


def hybrid_layer(x, xc, c, c_ctx, w_in, w_out, w_ada, b_ada, norm_w, dec_f, dec_b,
                 lam_q1, lam_k1, lam_q2, lam_k2, dif_norm_w, mlp_vnorm_w, mlp_ws, mlp_bs,
                 lambda_init, rope_ret, rope_dif, ctx_out):
    f32 = jnp.float32
    sh, sc, gt = modulation(c[:, None, :], w_ada, b_ada)
    sh_c, sc_c, gt_c = modulation(c_ctx[None, None, :], w_ada, b_ada)
    h = rms_norm(x, norm_w) * (1.0 + sc) + sh
    hc = rms_norm(xc, norm_w) * (1.0 + sc_c) + sh_c

    rk, rv, dk, dv, rq, dq, rg, dg, mu, mv, mg = split_cols(h @ w_in, KV_SPLITS + REST_SPLITS)
    if ctx_out:
        zc = split_cols(hc @ w_in, KV_SPLITS + REST_SPLITS)
    else:
        zc = split_cols(hc @ w_in[:, :KV_COLS], KV_SPLITS)
    rk_c, rv_c, dk_c, dv_c = zc[:4]

    ret_scale = RET_HEAD_DIM ** -0.5
    lg_f = jnp.log1p(-jnp.exp(dec_f.astype(f32)))
    lg_b = jnp.log1p(-jnp.exp(dec_b.astype(f32)))
    k_ch = heads_first(rk_c, RET_HEADS).astype(f32) * ret_scale
    v_ch = heads_first(rv_c, RET_HEADS).astype(f32)
    s_f, s_b = context_states(k_ch, v_ch, lg_f, lg_b)
    cos_r, sin_r = rope_ret
    q_h = apply_rope(heads_first(rq, RET_HEADS).astype(f32), cos_r, sin_r)
    k_h = apply_rope(heads_first(rk, RET_HEADS).astype(f32), cos_r, sin_r) * ret_scale
    v_h = heads_first(rv, RET_HEADS).astype(f32)
    ret = retention_out(bi_retention(q_h, k_h, v_h, lg_f, lg_b, s_f, s_b), rg)

    lam = (jnp.exp(jnp.sum(lam_q1.astype(f32) * lam_k1.astype(f32)))
           - jnp.exp(jnp.sum(lam_q2.astype(f32) * lam_k2.astype(f32))) + lambda_init)
    dif_scale = DIF_QK_DIM ** -0.5
    cos_d, sin_d = rope_dif
    q_d = apply_rope(dif_heads(dq), cos_d, sin_d) * dif_scale
    k_d = apply_rope(dif_heads(dk), cos_d, sin_d)
    v_d = dv.reshape(dv.shape[0], dv.shape[1], DIF_HEADS, DIF_V_DIM)
    k_dc = dif_heads(dk_c)
    v_dc = dv_c.reshape(dv_c.shape[0], dv_c.shape[1], DIF_HEADS, DIF_V_DIM)
    o_d = diff_attention(q_d, jnp.concatenate([k_d, k_dc], axis=1),
                         jnp.concatenate([v_d, v_dc], axis=1), lam)
    dif = diff_out(o_d, dif_norm_w, lambda_init, dg)

    mlp = chunk_mlp(jax.nn.gelu(mu, approximate=False), jax.nn.gelu(mv, approximate=False),
                    mlp_ws, mlp_bs, mlp_vnorm_w, mg)

    x = x + gt * (jnp.concatenate([ret, dif, mlp], axis=-1) @ w_out)

    if ctx_out:
        _, _, _, _, rq_c, dq_c, rg_c, dg_c, mu_c, mv_c, mg_c = zc
        q_ch = heads_first(rq_c, RET_HEADS).astype(f32)
        zero = jnp.zeros_like(s_f)
        ret_c = retention_out(bi_retention(q_ch, k_ch, v_ch, lg_f, lg_b, zero, zero), rg_c)
        o_dc = diff_attention(dif_heads(dq_c) * dif_scale, k_dc, v_dc, lam)
        dif_c = diff_out(o_dc, dif_norm_w, lambda_init, dg_c)
        mlp_c = chunk_mlp(jax.nn.gelu(mu_c, approximate=False), jax.nn.gelu(mv_c, approximate=False),
                          mlp_ws, mlp_bs, mlp_vnorm_w, mg_c)
        xc = xc + gt_c * (jnp.concatenate([ret_c, dif_c, mlp_c], axis=-1) @ w_out)
    return x, xc


def setup_inputs(seed: int = 0) -> dict:
    key = jax.random.key(seed)
    ks = jax.random.split(key, 20)
    f32 = jnp.float32

    def nrm(k, shape, scale):
        return jax.random.normal(k, shape, f32) * scale

    ret_base = -(5.0 + jnp.arange(RET_HEADS, dtype=f32)) * math.log(2.0)
    return {
        "x": nrm(ks[0], (BATCH, SEQ, D_MODEL), 1.0),
        "c": nrm(ks[1], (BATCH, D_MODEL), 1.0),
        "ctx": nrm(ks[2], (BATCH, CTX_LEN, D_MODEL), 1.0),
        "c_ctx": nrm(ks[3], (D_MODEL,), 1.0),
        "w_in": nrm(ks[4], (DEPTH, D_MODEL, IN_COLS), D_MODEL ** -0.5),
        "w_out": nrm(ks[5], (DEPTH, MIX_WIDTH, D_MODEL), MIX_WIDTH ** -0.5),
        "w_ada": nrm(ks[6], (DEPTH, D_MODEL, 3 * D_MODEL), 0.5 * D_MODEL ** -0.5),
        "b_ada": nrm(ks[7], (DEPTH, 3 * D_MODEL), 0.02),
        "norm_w": 1.0 + nrm(ks[8], (DEPTH, D_MODEL), 0.02),
        "ret_decay_fwd": ret_base + nrm(ks[9], (DEPTH, RET_HEADS), 0.05),
        "ret_decay_bwd": ret_base + nrm(ks[10], (DEPTH, RET_HEADS), 0.05),
        "lam_q1": nrm(ks[11], (DEPTH, DIF_QK_DIM), 0.1),
        "lam_k1": nrm(ks[12], (DEPTH, DIF_QK_DIM), 0.1),
        "lam_q2": nrm(ks[13], (DEPTH, DIF_QK_DIM), 0.1),
        "lam_k2": nrm(ks[14], (DEPTH, DIF_QK_DIM), 0.1),
        "dif_norm_w": 1.0 + nrm(ks[15], (DEPTH, DIF_V_DIM), 0.02),
        "mlp_vnorm_w": 1.0 + nrm(ks[16], (DEPTH, MLP_WIDTH), 0.02),
        "mlp_ws": nrm(ks[17], (DEPTH, MLP_GROUPS, CHUNK, CHUNK), CHUNK ** -0.5),
        "mlp_bs": 1.0 + nrm(ks[18], (DEPTH, MLP_GROUPS, CHUNK), 0.02),
        "final_norm_w": 1.0 + nrm(ks[19], (D_MODEL,), 0.02),
    }


def reference(x, c, ctx, c_ctx, w_in, w_out, w_ada, b_ada, norm_w, ret_decay_fwd, ret_decay_bwd,
              lam_q1, lam_k1, lam_q2, lam_k2, dif_norm_w, mlp_vnorm_w, mlp_ws, mlp_bs, final_norm_w):
    n = x.shape[1]
    rope_ret = axial_rope(n, RET_HEAD_DIM)
    cos_d, sin_d = axial_rope(n, DIF_QK_DIM)
    rope_dif = (cos_d[:, None, None, :], sin_d[:, None, None, :])
    xc = ctx
    for i in range(DEPTH):
        lambda_init = 0.8 - 0.6 * math.exp(-0.3 * i)
        x, xc = hybrid_layer(x, xc, c, c_ctx, w_in[i], w_out[i], w_ada[i], b_ada[i], norm_w[i],
                             ret_decay_fwd[i], ret_decay_bwd[i], lam_q1[i], lam_k1[i], lam_q2[i],
                             lam_k2[i], dif_norm_w[i], mlp_vnorm_w[i], mlp_ws[i], mlp_bs[i],
                             lambda_init, rope_ret, rope_dif, i < DEPTH - 1)
    return rms_norm(x, final_norm_w)
```

```python
import functools
import math

import jax
import jax.numpy as jnp
from jax import lax
from jax.experimental import pallas as pl
from jax.experimental.pallas import tpu as pltpu

HEAD_DIM = 128
CHUNK = 128
GRID_W = 64
ROPE_BASE = 10000.0
EPS = 1e-6
LOG2E = 1.4426950408889634
LANES = 128
NEG_BIG = -1e30
VMEM_LIMIT = 56 * 1024 * 1024

BF16 = jnp.bfloat16
F32 = jnp.float32

NT_DIMS = (((1,), (1,)), ((), ()))
TN_DIMS = (((0,), (0,)), ((), ()))


def _params(semantics):
    return pltpu.CompilerParams(dimension_semantics=semantics, vmem_limit_bytes=VMEM_LIMIT)


def _pick(n, candidates):
    for c in candidates:
        if n % c == 0:
            return c
    return n


def _silu(x):
    return x * (1.0 / (1.0 + jnp.exp(-x)))


def _gelu(x):
    return 0.5 * x * (1.0 + lax.erf(x * (2.0 ** -0.5)))


def _mod_kernel(cc_ref, w_ref, b_ref, o_ref):
    a = _silu(cc_ref[...]).astype(BF16)
    o_ref[...] = jnp.dot(a, w_ref[...].astype(BF16), preferred_element_type=F32) + b_ref[...]


def _modulation(cc, w_ada, b_ada):
    depth, d, n3 = w_ada.shape
    tn = _pick(n3, (512, 256, 128))
    return pl.pallas_call(
        _mod_kernel,
        grid=(depth, n3 // tn),
        in_specs=[
            pl.BlockSpec((8, d), lambda l, j: (0, 0)),
            pl.BlockSpec((None, d, tn), lambda l, j: (l, 0, j)),
            pl.BlockSpec((None, 1, tn), lambda l, j: (l, 0, j)),
        ],
        out_specs=pl.BlockSpec((None, 8, tn), lambda l, j: (l, 0, j)),
        out_shape=jax.ShapeDtypeStruct((depth, 8, n3), F32),
        compiler_params=_params(("arbitrary", "arbitrary")),
        name="modulation",
    )(cc, w_ada, b_ada.reshape(depth, 1, n3))


def _norm_mod_kernel(x_ref, nw_ref, mod_ref, h_ref, *, d, n_latent_blocks):
    x = x_ref[...]
    y = x * lax.rsqrt(jnp.mean(x * x, axis=-1, keepdims=True) + EPS) * nw_ref[...]
    is_ctx = pl.program_id(0) >= n_latent_blocks
    shift = jnp.where(is_ctx, mod_ref[1:2, 0:d], mod_ref[0:1, 0:d])
    scale = jnp.where(is_ctx, mod_ref[1:2, d:2 * d], mod_ref[0:1, d:2 * d])
    h_ref[...] = (y * (1.0 + scale) + shift).astype(h_ref.dtype)


def _norm_mod(xx, nw, mod, seq):
    m, d = xx.shape
    tr = 256
    return pl.pallas_call(
        functools.partial(_norm_mod_kernel, d=d, n_latent_blocks=seq // tr),
        grid=(m // tr,),
        in_specs=[
            pl.BlockSpec((tr, d), lambda i: (i, 0)),
            pl.BlockSpec((1, d), lambda i: (0, 0)),
            pl.BlockSpec((8, 3 * d), lambda i: (0, 0)),
        ],
        out_specs=pl.BlockSpec((tr, d), lambda i: (i, 0)),
        out_shape=jax.ShapeDtypeStruct((m, d), BF16),
        compiler_params=_params(("arbitrary",)),
        name="norm_mod",
    )(xx, nw, mod)


def _matmul_kernel(a_ref, b_ref, o_ref):
    o_ref[...] = jnp.dot(a_ref[...], b_ref[...], preferred_element_type=F32).astype(o_ref.dtype)


def _in_proj(h, w, out_dtype):
    m, k = h.shape
    n = w.shape[1]
    tm = _pick(m, (1056, 768, 640, 256))
    tn = _pick(n, (1024, 512, 256))
    return pl.pallas_call(
        _matmul_kernel,
        grid=(m // tm, n // tn),
        in_specs=[
            pl.BlockSpec((tm, k), lambda i, j: (i, 0)),
            pl.BlockSpec((k, tn), lambda i, j: (0, j)),
        ],
        out_specs=pl.BlockSpec((tm, tn), lambda i, j: (i, j)),
        out_shape=jax.ShapeDtypeStruct((m, n), out_dtype),
        compiler_params=_params(("arbitrary", "arbitrary")),
        name="in_proj",
    )(h, w)


def _retention_kernel(decf_ref, decb_ref, q_ref, k_ref, v_ref, g_ref, cos_ref, sin_ref, o_ref,
                      ob_ref, *, layer, n_latent_chunks, n_ctx_chunks):
    head = pl.program_id(0)
    lgf = jnp.log1p(-jnp.exp(jnp.full((1, LANES), decf_ref[layer, head], F32)))
    lgb = jnp.log1p(-jnp.exp(jnp.full((1, LANES), decb_ref[layer, head], F32)))
    ri = lax.broadcasted_iota(jnp.int32, (CHUNK, CHUNK), 0)
    ci = lax.broadcasted_iota(jnp.int32, (CHUNK, CHUNK), 1)
    rel = (ri - ci).astype(F32)
    pos = ri.astype(F32)
    dmat = jnp.where(rel >= 0, jnp.exp(lgf * jnp.maximum(rel, 0.0)), jnp.exp(lgb * jnp.maximum(-rel, 0.0)))
    wfk = jnp.exp(lgf * (CHUNK - 1.0 - pos))
    wbk = jnp.exp(lgb * pos)
    dqf = jnp.exp(lgf * (pos + 1.0))
    dqb = jnp.exp(lgb * (CHUNK - pos))
    cdf = jnp.exp(lgf * float(CHUNK))
    cdb = jnp.exp(lgb * float(CHUNK))
    ret_scale = HEAD_DIM ** -0.5

    def rows_of(c):
        return pl.ds(pl.multiple_of(c * CHUNK, CHUNK), CHUNK)

    def roped(rows):
        cos = cos_ref[rows, :]
        sin = sin_ref[rows, :]
        q = q_ref[rows, :].astype(F32)
        k = k_ref[rows, :].astype(F32)
        qr = q * cos + pltpu.roll(q, HEAD_DIM // 2, 1) * sin
        kr = (k * cos + pltpu.roll(k, HEAD_DIM // 2, 1) * sin) * ret_scale
        return qr, kr

    def run_sequence(first, n, sf, sb):
        def bwd(t, sb):
            rows = rows_of(first + n - 1 - t)
            qr, kr = roped(rows)
            vb = v_ref[rows, :].astype(BF16)
            ob_ref[rows, :] = jnp.dot((qr * dqb).astype(BF16), sb.astype(BF16), preferred_element_type=F32)
            kv = lax.dot_general((kr * wbk).astype(BF16), vb, TN_DIMS, preferred_element_type=F32)
            return sb * cdb + kv

        sb = lax.fori_loop(0, n, bwd, sb)

        def fwd(t, sf):
            rows = rows_of(first + t)
            qr, kr = roped(rows)
            vb = v_ref[rows, :].astype(BF16)
            a = lax.dot_general(qr.astype(BF16), kr.astype(BF16), NT_DIMS, preferred_element_type=F32) * dmat
            o = jnp.dot(a.astype(BF16), vb, preferred_element_type=F32)
            o = o + jnp.dot((qr * dqf).astype(BF16), sf.astype(BF16), preferred_element_type=F32)
            o = o + ob_ref[rows, :]
            o = o * lax.rsqrt(jnp.mean(o * o, axis=-1, keepdims=True) + EPS)
            o_ref[rows, :] = (o * _silu(g_ref[rows, :].astype(F32))).astype(o_ref.dtype)
            kv = lax.dot_general((kr * wfk).astype(BF16), vb, TN_DIMS, preferred_element_type=F32)
            return sf * cdf + kv

        sf = lax.fori_loop(0, n, fwd, sf)
        return sf, sb

    zero = jnp.zeros((HEAD_DIM, HEAD_DIM), F32)
    sf, sb = run_sequence(n_latent_chunks, n_ctx_chunks, zero, zero)
    run_sequence(0, n_latent_chunks, sf, sb)


def _retention(z, dec_f, dec_b, cos, sin, *, layer, heads, seq, q_blk, k_blk, v_blk, g_blk):
    m = z.shape[0]
    col = lambda base: pl.BlockSpec((m, HEAD_DIM), lambda h: (0, base + h))
    smem = pl.BlockSpec(memory_space=pltpu.SMEM)
    full = pl.BlockSpec((m, HEAD_DIM), lambda h: (0, 0), pipeline_mode=pl.Buffered(1))
    return pl.pallas_call(
        functools.partial(_retention_kernel, layer=layer, n_latent_chunks=seq // CHUNK,
                          n_ctx_chunks=(m - seq) // CHUNK),
        grid=(heads,),
        in_specs=[smem, smem, col(q_blk), col(k_blk), col(v_blk), col(g_blk), full, full],
        out_specs=pl.BlockSpec((m, HEAD_DIM), lambda h: (0, h)),
        out_shape=jax.ShapeDtypeStruct((m, heads * HEAD_DIM), BF16),
        scratch_shapes=[pltpu.VMEM((m, HEAD_DIM), F32)],
        compiler_params=_params(("arbitrary",)),
        name="retention",
    )(dec_f, dec_b, z, z, z, z, cos, sin)


def _swap_half_pairs(x):
    lane = lax.broadcasted_iota(jnp.int32, x.shape, 1)
    return jnp.where(lane % 64 < 32, pltpu.roll(x, 96, 1), pltpu.roll(x, 32, 1))


def _dif_rope(x, cos, sin):
    return x * cos + _swap_half_pairs(x) * sin


def _attention_kernel(lq1_ref, lk1_ref, lq2_ref, lk2_ref, q_ref, cosq_ref, sinq_ref, k_ref, v_ref,
                      cosk_ref, sink_ref, g_ref, nw_ref, o_ref, ks_ref, vs_ref,
                      *, lambda_init, seq, tq, tk, prep_rows):
    qi = pl.program_id(1)
    m = k_ref.shape[0]
    ctx = m - seq

    @pl.when(qi == 0)
    def _():
        def prep(r, carry):
            rows = pl.ds(pl.multiple_of(r * prep_rows, prep_rows), prep_rows)
            k = _dif_rope(k_ref[rows, :].astype(F32), cosk_ref[rows, :], sink_ref[rows, :])
            ks_ref[rows, :] = k.astype(BF16)
            vs_ref[rows, :] = v_ref[rows, :].astype(BF16)
            return carry
        lax.fori_loop(0, m // prep_rows, prep, 0)

    q = _dif_rope(q_ref[...].astype(F32), cosq_ref[...], sinq_ref[...]) * ((HEAD_DIM // 2) ** -0.5 * LOG2E)
    lane = lax.broadcasted_iota(jnp.int32, q.shape, 1)
    q0 = jnp.where(lane < HEAD_DIM // 2, q, 0.0).astype(BF16)
    q1 = jnp.where(lane >= HEAD_DIM // 2, q, 0.0).astype(BF16)

    def component(kb, vb, qm, state):
        mx, l, acc = state
        s = lax.dot_general(kb, qm, NT_DIMS, preferred_element_type=F32)
        mn = jnp.maximum(mx, jnp.max(s, axis=0, keepdims=True))
        p = jnp.exp2(s - mn)
        a = jnp.exp2(mx - mn)
        l = a * l + jnp.sum(p, axis=0, keepdims=True)
        acc = a * acc + lax.dot_general(vb, p.astype(BF16), TN_DIMS, preferred_element_type=F32)
        return mn, l, acc

    def update(rows, carry):
        kb = ks_ref[rows, :]
        vb = vs_ref[rows, :]
        return component(kb, vb, q0, carry[0]), component(kb, vb, q1, carry[1])

    def init():
        return (jnp.full((1, tq), NEG_BIG, F32), jnp.zeros((1, tq), F32), jnp.zeros((HEAD_DIM, tq), F32))

    def body(c, carry):
        return update(pl.ds(pl.multiple_of(c * tk, tk), tk), carry)

    n_latent_chunks = jnp.where(qi < seq // tq, seq // tk, 0)
    carry = lax.fori_loop(0, n_latent_chunks, body, (init(), init()))
    (_, l0, acc0), (_, l1, acc1) = update(pl.ds(seq, ctx), carry)

    lam = (jnp.exp(jnp.sum(lq1_ref[...] * lk1_ref[...], axis=-1, keepdims=True))
           - jnp.exp(jnp.sum(lq2_ref[...] * lk2_ref[...], axis=-1, keepdims=True)) + lambda_init)
    ot = acc0 * (1.0 / l0) - lam * (acc1 * (1.0 / l1))
    o = ot.T
    o = o * lax.rsqrt(jnp.mean(o * o, axis=-1, keepdims=True) + EPS) * nw_ref[...] * (1.0 - lambda_init)
    o_ref[...] = (o * _silu(g_ref[...].astype(F32))).astype(o_ref.dtype)


def _attention(z, lams, cos, sin, nw, *, lambda_init, heads, seq, q_blk, k_blk, v_blk, g_blk):
    m = z.shape[0]
    ctx = m - seq
    tq = 256
    tk = _pick(seq, (512, 256, 128))
    assert seq % tq == 0 and ctx == tq, "one query block must hold exactly the context rows"
    prep_rows = 256
    lam_spec = pl.BlockSpec((1, HEAD_DIM // 2), lambda h, i: (0, 0))
    qrow = lambda base: pl.BlockSpec((tq, HEAD_DIM), lambda h, i: (i, base + h))
    kcol = lambda base: pl.BlockSpec((m, HEAD_DIM), lambda h, i: (0, base + h))
    return pl.pallas_call(
        functools.partial(_attention_kernel, lambda_init=lambda_init, seq=seq, tq=tq, tk=tk,
                          prep_rows=prep_rows),
        grid=(heads, m // tq),
        in_specs=[lam_spec, lam_spec, lam_spec, lam_spec,
                  qrow(q_blk),
                  pl.BlockSpec((tq, HEAD_DIM), lambda h, i: (i, 0)),
                  pl.BlockSpec((tq, HEAD_DIM), lambda h, i: (i, 0)),
                  kcol(k_blk), kcol(v_blk),
                  pl.BlockSpec((m, HEAD_DIM), lambda h, i: (0, 0), pipeline_mode=pl.Buffered(1)),
                  pl.BlockSpec((m, HEAD_DIM), lambda h, i: (0, 0), pipeline_mode=pl.Buffered(1)),
                  qrow(g_blk),
                  pl.BlockSpec((1, HEAD_DIM), lambda h, i: (0, 0))],
        out_specs=pl.BlockSpec((tq, HEAD_DIM), lambda h, i: (i, h)),
        out_shape=jax.ShapeDtypeStruct((m, heads * HEAD_DIM), BF16),
        scratch_shapes=[pltpu.VMEM((m, HEAD_DIM), BF16), pltpu.VMEM((m, HEAD_DIM), BF16)],
        compiler_params=_params(("arbitrary", "arbitrary")),
        name="diff_attention",
    )(*lams, z, cos, sin, z, z, cos, sin, z, nw)


def _mlp_kernel(u_ref, v_ref, g_ref, vnw_ref, ws_ref, bst_ref, o_ref, *, groups):
    tr, width = u_ref.shape
    gd = width // groups
    v = _gelu(v_ref[...].astype(F32))
    v = (v * lax.rsqrt(jnp.mean(v * v, axis=-1, keepdims=True) + EPS) * vnw_ref[...]).astype(BF16)
    for c in range(tr // CHUNK):
        rows = slice(c * CHUNK, (c + 1) * CHUNK)
        for g in range(groups):
            cols = slice(g * gd, (g + 1) * gd)
            mixed = jnp.dot(ws_ref[g].astype(BF16), v[rows, cols], preferred_element_type=F32) + bst_ref[:, g:g + 1]
            u = _gelu(u_ref[rows, cols].astype(F32))
            o_ref[rows, cols] = (u * mixed * _silu(g_ref[rows, cols].astype(F32))).astype(o_ref.dtype)


def _chunk_mlp(z, vnw, ws, bst, *, u_blk, width):
    m = z.shape[0]
    groups = ws.shape[0]
    tr = 256
    col = lambda b: pl.BlockSpec((tr, width), lambda i: (i, b))
    return pl.pallas_call(
        functools.partial(_mlp_kernel, groups=groups),
        grid=(m // tr,),
        in_specs=[col(u_blk), col(u_blk + 1), col(u_blk + 2),
                  pl.BlockSpec((1, width), lambda i: (0, 0)),
                  pl.BlockSpec((groups, CHUNK, CHUNK), lambda i: (0, 0, 0)),
                  pl.BlockSpec((CHUNK, groups), lambda i: (0, 0))],
        out_specs=pl.BlockSpec((tr, width), lambda i: (i, 0)),
        out_shape=jax.ShapeDtypeStruct((m, width), BF16),
        compiler_params=_params(("arbitrary",)),
        name="chunk_mlp",
    )(z, z, z, vnw, ws, bst)


def _out_proj_kernel(ret_ref, dif_ref, mlp_ref, wr_ref, wd_ref, wm_ref, x_ref, gate_ref, o_ref, *, seq, tm):
    y = jnp.dot(ret_ref[...], wr_ref[...], preferred_element_type=F32)
    y = y + jnp.dot(dif_ref[...], wd_ref[...], preferred_element_type=F32)
    y = y + jnp.dot(mlp_ref[...], wm_ref[...], preferred_element_type=F32)
    row = pl.program_id(0) * tm + lax.broadcasted_iota(jnp.int32, y.shape, 0)
    gate = jnp.where(row >= seq, gate_ref[1:2, :], gate_ref[0:1, :])
    o_ref[...] = x_ref[...] + gate * y


def _out_proj(ret, dif, mlp, w_out, xx, mod, *, seq):
    m, d = xx.shape
    rw, dw, mw = ret.shape[1], dif.shape[1], mlp.shape[1]
    assert rw == dw and (rw + dw) % mw == 0
    tm = _pick(m, (1056, 768, 640, 256))
    tn = _pick(d, (512, 256))
    gate_blk = 2 * d // tn
    return pl.pallas_call(
        functools.partial(_out_proj_kernel, seq=seq, tm=tm),
        grid=(m // tm, d // tn),
        in_specs=[
            pl.BlockSpec((tm, rw), lambda i, j: (i, 0)),
            pl.BlockSpec((tm, dw), lambda i, j: (i, 0)),
            pl.BlockSpec((tm, mw), lambda i, j: (i, 0)),
            pl.BlockSpec((rw, tn), lambda i, j: (0, j)),
            pl.BlockSpec((dw, tn), lambda i, j: (1, j)),
            pl.BlockSpec((mw, tn), lambda i, j: ((rw + dw) // mw, j)),
            pl.BlockSpec((tm, tn), lambda i, j: (i, j)),
            pl.BlockSpec((8, tn), lambda i, j: (0, gate_blk + j)),
        ],
        out_specs=pl.BlockSpec((tm, tn), lambda i, j: (i, j)),
        out_shape=jax.ShapeDtypeStruct((m, d), F32),
        compiler_params=_params(("arbitrary", "arbitrary")),
        name="out_proj",
    )(ret, dif, mlp, w_out, w_out, w_out, xx, mod)


def _final_norm_kernel(x_ref, w_ref, o_ref):
    x = x_ref[...]
    o_ref[...] = x * lax.rsqrt(jnp.mean(x * x, axis=-1, keepdims=True) + EPS) * w_ref[...]


def _final_norm(xx, w, seq):
    d = xx.shape[1]
    tr = 256
    return pl.pallas_call(
        _final_norm_kernel,
        grid=(seq // tr,),
        in_specs=[pl.BlockSpec((tr, d), lambda i: (i, 0)), pl.BlockSpec((1, d), lambda i: (0, 0))],
        out_specs=pl.BlockSpec((tr, d), lambda i: (i, 0)),
        out_shape=jax.ShapeDtypeStruct((seq, d), F32),
        compiler_params=_params(("arbitrary",)),
        name="final_norm",
    )(xx, w)


def _rope_tables(seq, ctx, dim, repeats):
    t = jnp.arange(seq, dtype=jnp.int32)
    r = (t // GRID_W).astype(F32)[:, None]
    c = (t % GRID_W).astype(F32)[:, None]
    n_freq = dim // 4
    inv = ROPE_BASE ** (-jnp.arange(n_freq, dtype=F32) / n_freq)
    ang = jnp.concatenate([r * inv, c * inv], axis=-1)
    cos = jnp.concatenate([jnp.cos(ang), jnp.cos(ang)], axis=-1)
    sin = jnp.concatenate([-jnp.sin(ang), jnp.sin(ang)], axis=-1)
    cos = jnp.concatenate([jnp.tile(cos, (1, repeats)), jnp.ones((ctx, repeats * dim), F32)], axis=0)
    sin = jnp.concatenate([jnp.tile(sin, (1, repeats)), jnp.zeros((ctx, repeats * dim), F32)], axis=0)
    return cos, sin


def kernel(x, c, ctx, c_ctx, w_in, w_out, w_ada, b_ada, norm_w, ret_decay_fwd, ret_decay_bwd,
           lam_q1, lam_k1, lam_q2, lam_k2, dif_norm_w, mlp_vnorm_w, mlp_ws, mlp_bs, final_norm_w):
    batch, seq, d = x.shape
    assert batch == 1, "the kernels process one sequence"
    n_ctx = ctx.shape[1]
    depth = w_in.shape[0]
    heads = ret_decay_fwd.shape[1]
    ret_w = heads * HEAD_DIM
    dif_w = ret_w
    mlp_w = mlp_vnorm_w.shape[1]
    assert 2 * ret_w + mlp_w == d and w_in.shape[2] == 8 * ret_w + 3 * mlp_w

    rk_blk, rv_blk, dk_blk, dv_blk = 0, heads, 2 * heads, 3 * heads
    rq_blk, dq_blk, rg_blk, dg_blk = 4 * heads, 5 * heads, 6 * heads, 7 * heads
    mu_blk = 8 * ret_w // mlp_w

    xx = jnp.concatenate([x[0], ctx[0]], axis=0)
    cc = jnp.zeros((8, d), F32).at[0].set(c[0]).at[1].set(c_ctx)
    mods = _modulation(cc, w_ada, b_ada)
    cos_r, sin_r = _rope_tables(seq, n_ctx, HEAD_DIM, 1)
    cos_d, sin_d = _rope_tables(seq, n_ctx, HEAD_DIM // 2, 2)
    w_in_b = w_in.astype(BF16)
    w_out_b = w_out.astype(BF16)

    for l in range(depth):
        lambda_init = 0.8 - 0.6 * math.exp(-0.3 * l)
        h = _norm_mod(xx, norm_w[l][None, :], mods[l], seq)
        z = _in_proj(h, w_in_b[l], BF16)
        ret = _retention(z, ret_decay_fwd, ret_decay_bwd, cos_r, sin_r, layer=l, heads=heads, seq=seq,
                         q_blk=rq_blk, k_blk=rk_blk, v_blk=rv_blk, g_blk=rg_blk)
        lams = [a[l][None, :] for a in (lam_q1, lam_k1, lam_q2, lam_k2)]
        dif = _attention(z, lams, cos_d, sin_d, dif_norm_w[l][None, :], lambda_init=lambda_init,
                         heads=heads, seq=seq, q_blk=dq_blk, k_blk=dk_blk, v_blk=dv_blk, g_blk=dg_blk)
        mlp = _chunk_mlp(z, mlp_vnorm_w[l][None, :], mlp_ws[l], mlp_bs[l].T, u_blk=mu_blk, width=mlp_w)
        xx = _out_proj(ret, dif, mlp, w_out_b[l], xx, mods[l], seq=seq)

    return _final_norm(xx, final_norm_w[None, :], seq)[None]
```

```python
import functools
import math

import jax
import jax.numpy as jnp
from jax import lax
from jax.experimental import pallas as pl
from jax.experimental.pallas import tpu as pltpu

HEAD_DIM = 128
CHUNK = 128
GRID_W = 64
ROPE_BASE = 10000.0
EPS = 1e-6
LOG2E = 1.4426950408889634
LANES = 128
NEG_BIG = -1e30
VMEM_LIMIT = 56 * 1024 * 1024

BF16 = jnp.bfloat16
F32 = jnp.float32

NT_DIMS = (((1,), (1,)), ((), ()))
TN_DIMS = (((0,), (0,)), ((), ()))


def _params(semantics):
    return pltpu.CompilerParams(dimension_semantics=semantics, vmem_limit_bytes=VMEM_LIMIT)


def _pick(n, candidates):
    for c in candidates:
        if n % c == 0:
            return c
    return n


def _silu(x):
    return x * (1.0 / (1.0 + jnp.exp(-x)))


def _gelu(x):
    return 0.5 * x * (1.0 + lax.erf(x * (2.0 ** -0.5)))


def _mod_kernel(cc_ref, w_ref, b_ref, o_ref):
    a = _silu(cc_ref[...]).astype(BF16)
    o_ref[...] = jnp.dot(a, w_ref[...].astype(BF16), preferred_element_type=F32) + b_ref[...]


def _modulation(cc, w_ada, b_ada):
    depth, d, n3 = w_ada.shape
    tn = _pick(n3, (512, 256, 128))
    return pl.pallas_call(
        _mod_kernel,
        grid=(depth, n3 // tn),
        in_specs=[
            pl.BlockSpec((8, d), lambda l, j: (0, 0)),
            pl.BlockSpec((None, d, tn), lambda l, j: (l, 0, j)),
            pl.BlockSpec((None, 1, tn), lambda l, j: (l, 0, j)),
        ],
        out_specs=pl.BlockSpec((None, 8, tn), lambda l, j: (l, 0, j)),
        out_shape=jax.ShapeDtypeStruct((depth, 8, n3), F32),
        compiler_params=_params(("arbitrary", "arbitrary")),
        name="modulation",
    )(cc, w_ada, b_ada.reshape(depth, 1, n3))


def _norm_mod_kernel(x_ref, nw_ref, mod_ref, h_ref, *, d, n_latent_blocks):
    x = x_ref[...]
    y = x * lax.rsqrt(jnp.mean(x * x, axis=-1, keepdims=True) + EPS) * nw_ref[...]
    is_ctx = pl.program_id(0) >= n_latent_blocks
    shift = jnp.where(is_ctx, mod_ref[1:2, 0:d], mod_ref[0:1, 0:d])
    scale = jnp.where(is_ctx, mod_ref[1:2, d:2 * d], mod_ref[0:1, d:2 * d])
    h_ref[...] = (y * (1.0 + scale) + shift).astype(h_ref.dtype)


def _norm_mod(xx, nw, mod, seq):
    m, d = xx.shape
    tr = 256
    return pl.pallas_call(
        functools.partial(_norm_mod_kernel, d=d, n_latent_blocks=seq // tr),
        grid=(m // tr,),
        in_specs=[
            pl.BlockSpec((tr, d), lambda i: (i, 0)),
            pl.BlockSpec((1, d), lambda i: (0, 0)),
            pl.BlockSpec((8, 3 * d), lambda i: (0, 0)),
        ],
        out_specs=pl.BlockSpec((tr, d), lambda i: (i, 0)),
        out_shape=jax.ShapeDtypeStruct((m, d), BF16),
        compiler_params=_params(("arbitrary",)),
        name="norm_mod",
    )(xx, nw, mod)


def _matmul_kernel(a_ref, b_ref, o_ref):
    o_ref[...] = jnp.dot(a_ref[...], b_ref[...], preferred_element_type=F32).astype(o_ref.dtype)


def _in_proj(h, w, out_dtype):
    m, k = h.shape
    n = w.shape[1]
    tm = _pick(m, (1056, 768, 640, 256))
    tn = _pick(n, (1024, 512, 256))
    return pl.pallas_call(
        _matmul_kernel,
        grid=(m // tm, n // tn),
        in_specs=[
            pl.BlockSpec((tm, k), lambda i, j: (i, 0)),
            pl.BlockSpec((k, tn), lambda i, j: (0, j)),
        ],
        out_specs=pl.BlockSpec((tm, tn), lambda i, j: (i, j)),
        out_shape=jax.ShapeDtypeStruct((m, n), out_dtype),
        compiler_params=_params(("arbitrary", "arbitrary")),
        name="in_proj",
    )(h, w)


def _retention_kernel(decf_ref, decb_ref, q_ref, k_ref, v_ref, g_ref, cos_ref, sin_ref, o_ref,
                      ob_ref, *, layer, n_latent_chunks, n_ctx_chunks):
    head = pl.program_id(0)
    lgf = jnp.log1p(-jnp.exp(jnp.full((1, LANES), decf_ref[layer, head], F32)))
    lgb = jnp.log1p(-jnp.exp(jnp.full((1, LANES), decb_ref[layer, head], F32)))
    ri = lax.broadcasted_iota(jnp.int32, (CHUNK, CHUNK), 0)
    ci = lax.broadcasted_iota(jnp.int32, (CHUNK, CHUNK), 1)
    rel = (ri - ci).astype(F32)
    pos = ri.astype(F32)
    dmat = jnp.where(rel >= 0, jnp.exp(lgf * jnp.maximum(rel, 0.0)), jnp.exp(lgb * jnp.maximum(-rel, 0.0)))
    wfk = jnp.exp(lgf * (CHUNK - 1.0 - pos))
    wbk = jnp.exp(lgb * pos)
    dqf = jnp.exp(lgf * (pos + 1.0))
    dqb = jnp.exp(lgb * (CHUNK - pos))
    cdf = jnp.exp(lgf * float(CHUNK))
    cdb = jnp.exp(lgb * float(CHUNK))
    ret_scale = HEAD_DIM ** -0.5

    def rows_of(c):
        return pl.ds(pl.multiple_of(c * CHUNK, CHUNK), CHUNK)

    def roped(rows):
        cos = cos_ref[rows, :]
        sin = sin_ref[rows, :]
        q = q_ref[rows, :].astype(F32)
        k = k_ref[rows, :].astype(F32)
        qr = q * cos + pltpu.roll(q, HEAD_DIM // 2, 1) * sin
        kr = (k * cos + pltpu.roll(k, HEAD_DIM // 2, 1) * sin) * ret_scale
        return qr, kr

    def run_sequence(first, n, sf, sb):
        def bwd(t, sb):
            rows = rows_of(first + n - 1 - t)
            qr, kr = roped(rows)
            vb = v_ref[rows, :].astype(BF16)
            ob_ref[rows, :] = jnp.dot((qr * dqb).astype(BF16), sb.astype(BF16), preferred_element_type=F32)
            kv = lax.dot_general((kr * wbk).astype(BF16), vb, TN_DIMS, preferred_element_type=F32)
            return sb * cdb + kv

        sb = lax.fori_loop(0, n, bwd, sb)

        def fwd(t, sf):
            rows = rows_of(first + t)
            qr, kr = roped(rows)
            vb = v_ref[rows, :].astype(BF16)
            a = lax.dot_general(qr.astype(BF16), kr.astype(BF16), NT_DIMS, preferred_element_type=F32) * dmat
            o = jnp.dot(a.astype(BF16), vb, preferred_element_type=F32)
            o = o + jnp.dot((qr * dqf).astype(BF16), sf.astype(BF16), preferred_element_type=F32)
            o = o + ob_ref[rows, :]
            o = o * lax.rsqrt(jnp.mean(o * o, axis=-1, keepdims=True) + EPS)
            o_ref[rows, :] = (o * _silu(g_ref[rows, :].astype(F32))).astype(o_ref.dtype)
            kv = lax.dot_general((kr * wfk).astype(BF16), vb, TN_DIMS, preferred_element_type=F32)
            return sf * cdf + kv

        sf = lax.fori_loop(0, n, fwd, sf)
        return sf, sb

    zero = jnp.zeros((HEAD_DIM, HEAD_DIM), F32)
    sf, sb = run_sequence(n_latent_chunks, n_ctx_chunks, zero, zero)
    run_sequence(0, n_latent_chunks, sf, sb)


def _retention(z, dec_f, dec_b, cos, sin, *, layer, heads, seq, q_blk, k_blk, v_blk, g_blk):
    m = z.shape[0]
    col = lambda base: pl.BlockSpec((m, HEAD_DIM), lambda h: (0, base + h))
    smem = pl.BlockSpec(memory_space=pltpu.SMEM)
    full = pl.BlockSpec((m, HEAD_DIM), lambda h: (0, 0), pipeline_mode=pl.Buffered(1))
    return pl.pallas_call(
        functools.partial(_retention_kernel, layer=layer, n_latent_chunks=seq // CHUNK,
                          n_ctx_chunks=(m - seq) // CHUNK),
        grid=(heads,),
        in_specs=[smem, smem, col(q_blk), col(k_blk), col(v_blk), col(g_blk), full, full],
        out_specs=pl.BlockSpec((m, HEAD_DIM), lambda h: (0, h)),
        out_shape=jax.ShapeDtypeStruct((m, heads * HEAD_DIM), BF16),
        scratch_shapes=[pltpu.VMEM((m, HEAD_DIM), F32)],
        compiler_params=_params(("arbitrary",)),
        name="retention",
    )(dec_f, dec_b, z, z, z, z, cos, sin)


def _swap_half_pairs(x):
    lane = lax.broadcasted_iota(jnp.int32, x.shape, 1)
    return jnp.where(lane % 64 < 32, pltpu.roll(x, 96, 1), pltpu.roll(x, 32, 1))


def _dif_rope(x, cos, sin):
    return x * cos + _swap_half_pairs(x) * sin


def _attention_kernel(lq1_ref, lk1_ref, lq2_ref, lk2_ref, q_ref, cosq_ref, sinq_ref, k_ref, v_ref,
                      cosk_ref, sink_ref, g_ref, nw_ref, o_ref, ks_ref, vtx_ref, vtc_ref, s_ref,
                      *, lambda_init, seq, tq, tk):
    qi = pl.program_id(1)
    m = k_ref.shape[0]
    ctx = m - seq
    vrows = vtc_ref.shape[0]
    ones_rows = (lax.broadcasted_iota(jnp.int32, (vrows - HEAD_DIM, 1), 0) == 0).astype(BF16)

    @pl.when(qi == 0)
    def _():
        def prep_rows(rows, n):
            k = _dif_rope(k_ref[rows, :].astype(F32), cosk_ref[rows, :], sink_ref[rows, :])
            ks_ref[rows, :] = k.astype(BF16)
            vt = v_ref[rows, :].astype(F32).T.astype(BF16)
            return jnp.concatenate([vt, jnp.broadcast_to(ones_rows, (vrows - HEAD_DIM, n))], axis=0)

        def prep(c, carry):
            vtx_ref[c] = prep_rows(pl.ds(pl.multiple_of(c * tk, tk), tk), tk)
            return carry
        lax.fori_loop(0, seq // tk, prep, 0)
        vtc_ref[...] = prep_rows(pl.ds(seq, ctx), ctx)

    q = _dif_rope(q_ref[...].astype(F32), cosq_ref[...], sinq_ref[...]) * ((HEAD_DIM // 2) ** -0.5 * LOG2E)
    lane = lax.broadcasted_iota(jnp.int32, q.shape, 1)
    q0 = jnp.where(lane < HEAD_DIM // 2, q, 0.0).astype(BF16)
    q1 = jnp.where(lane >= HEAD_DIM // 2, q, 0.0).astype(BF16)

    def scores(rows, mx):
        kb = ks_ref[rows, :]
        out = []
        for comp, qm in enumerate((q0, q1)):
            s = lax.dot_general(kb, qm, NT_DIMS, preferred_element_type=F32)
            s_ref[comp, rows, :] = s
            out.append(jnp.maximum(mx[comp], jnp.max(s, axis=0, keepdims=True)))
        return tuple(out)

    def weighted_values(rows, vt, mx, acc):
        out = []
        for comp in range(2):
            p = jnp.exp2(s_ref[comp, rows, :] - mx[comp]).astype(BF16)
            out.append(acc[comp] + jnp.dot(vt, p, preferred_element_type=F32))
        return tuple(out)

    def chunk_rows(c):
        return pl.ds(pl.multiple_of(c * tk, tk), tk)

    n_latent_chunks = jnp.where(qi < seq // tq, seq // tk, 0)
    ctx_rows = pl.ds(seq, ctx)
    mx = (jnp.full((1, tq), NEG_BIG, F32),) * 2
    mx = lax.fori_loop(0, n_latent_chunks, lambda c, mx: scores(chunk_rows(c), mx), mx)
    mx = scores(ctx_rows, mx)
    acc = (jnp.zeros((vrows, tq), F32),) * 2
    acc = lax.fori_loop(0, n_latent_chunks,
                        lambda c, acc: weighted_values(chunk_rows(c), vtx_ref[c], mx, acc), acc)
    acc0, acc1 = weighted_values(ctx_rows, vtc_ref[...], mx, acc)

    lam = (jnp.exp(jnp.sum(lq1_ref[...] * lk1_ref[...], axis=-1, keepdims=True))
           - jnp.exp(jnp.sum(lq2_ref[...] * lk2_ref[...], axis=-1, keepdims=True)) + lambda_init)
    l0 = acc0[HEAD_DIM:HEAD_DIM + 1, :]
    l1 = acc1[HEAD_DIM:HEAD_DIM + 1, :]
    ot = acc0[:HEAD_DIM, :] * (1.0 / l0) - lam * (acc1[:HEAD_DIM, :] * (1.0 / l1))
    o = ot.T
    o = o * lax.rsqrt(jnp.mean(o * o, axis=-1, keepdims=True) + EPS) * nw_ref[...] * (1.0 - lambda_init)
    o_ref[...] = (o * _silu(g_ref[...].astype(F32))).astype(o_ref.dtype)


def _attention(z, lams, cos, sin, nw, *, lambda_init, heads, seq, q_blk, k_blk, v_blk, g_blk):
    m = z.shape[0]
    ctx = m - seq
    tq = 256
    tk = _pick(seq, (2048, 1024, 512, 256, 128))
    assert seq % tq == 0 and ctx == tq, "one query block must hold exactly the context rows"
    vrows = HEAD_DIM + 16
    lam_spec = pl.BlockSpec((1, HEAD_DIM // 2), lambda h, i: (0, 0))
    qrow = lambda base: pl.BlockSpec((tq, HEAD_DIM), lambda h, i: (i, base + h))
    kcol = lambda base: pl.BlockSpec((m, HEAD_DIM), lambda h, i: (0, base + h))
    table = pl.BlockSpec((m, HEAD_DIM), lambda h, i: (0, 0), pipeline_mode=pl.Buffered(1))
    return pl.pallas_call(
        functools.partial(_attention_kernel, lambda_init=lambda_init, seq=seq, tq=tq, tk=tk),
        grid=(heads, m // tq),
        in_specs=[lam_spec, lam_spec, lam_spec, lam_spec,
                  qrow(q_blk),
                  pl.BlockSpec((tq, HEAD_DIM), lambda h, i: (i, 0)),
                  pl.BlockSpec((tq, HEAD_DIM), lambda h, i: (i, 0)),
                  kcol(k_blk), kcol(v_blk), table, table,
                  qrow(g_blk),
                  pl.BlockSpec((1, HEAD_DIM), lambda h, i: (0, 0))],
        out_specs=pl.BlockSpec((tq, HEAD_DIM), lambda h, i: (i, h)),
        out_shape=jax.ShapeDtypeStruct((m, heads * HEAD_DIM), BF16),
        scratch_shapes=[pltpu.VMEM((m, HEAD_DIM), BF16),
                        pltpu.VMEM((seq // tk, vrows, tk), BF16),
                        pltpu.VMEM((vrows, ctx), BF16),
                        pltpu.VMEM((2, m, tq), F32)],
        compiler_params=_params(("arbitrary", "arbitrary")),
        name="diff_attention",
    )(*lams, z, cos, sin, z, z, cos, sin, z, nw)


def _mlp_kernel(u_ref, v_ref, g_ref, vnw_ref, ws_ref, bst_ref, o_ref, *, groups):
    tr, width = u_ref.shape
    gd = width // groups
    v = _gelu(v_ref[...].astype(F32))
    v = (v * lax.rsqrt(jnp.mean(v * v, axis=-1, keepdims=True) + EPS) * vnw_ref[...]).astype(BF16)
    for c in range(tr // CHUNK):
        rows = slice(c * CHUNK, (c + 1) * CHUNK)
        for g in range(groups):
            cols = slice(g * gd, (g + 1) * gd)
            mixed = jnp.dot(ws_ref[g].astype(BF16), v[rows, cols], preferred_element_type=F32) + bst_ref[:, g:g + 1]
            u = _gelu(u_ref[rows, cols].astype(F32))
            o_ref[rows, cols] = (u * mixed * _silu(g_ref[rows, cols].astype(F32))).astype(o_ref.dtype)


def _chunk_mlp(z, vnw, ws, bst, *, u_blk, width):
    m = z.shape[0]
    groups = ws.shape[0]
    tr = 256
    col = lambda b: pl.BlockSpec((tr, width), lambda i: (i, b))
    return pl.pallas_call(
        functools.partial(_mlp_kernel, groups=groups),
        grid=(m // tr,),
        in_specs=[col(u_blk), col(u_blk + 1), col(u_blk + 2),
                  pl.BlockSpec((1, width), lambda i: (0, 0)),
                  pl.BlockSpec((groups, CHUNK, CHUNK), lambda i: (0, 0, 0)),
                  pl.BlockSpec((CHUNK, groups), lambda i: (0, 0))],
        out_specs=pl.BlockSpec((tr, width), lambda i: (i, 0)),
        out_shape=jax.ShapeDtypeStruct((m, width), BF16),
        compiler_params=_params(("arbitrary",)),
        name="chunk_mlp",
    )(z, z, z, vnw, ws, bst)


def _out_proj_kernel(ret_ref, dif_ref, mlp_ref, wr_ref, wd_ref, wm_ref, x_ref, gate_ref, o_ref, *, seq, tm):
    y = jnp.dot(ret_ref[...], wr_ref[...], preferred_element_type=F32)
    y = y + jnp.dot(dif_ref[...], wd_ref[...], preferred_element_type=F32)
    y = y + jnp.dot(mlp_ref[...], wm_ref[...], preferred_element_type=F32)
    row = pl.program_id(0) * tm + lax.broadcasted_iota(jnp.int32, y.shape, 0)
    gate = jnp.where(row >= seq, gate_ref[1:2, :], gate_ref[0:1, :])
    o_ref[...] = x_ref[...] + gate * y


def _out_proj(ret, dif, mlp, w_out, xx, mod, *, seq):
    m, d = xx.shape
    rw, dw, mw = ret.shape[1], dif.shape[1], mlp.shape[1]
    assert rw == dw and (rw + dw) % mw == 0
    tm = _pick(m, (1056, 768, 640, 256))
    tn = _pick(d, (512, 256))
    gate_blk = 2 * d // tn
    return pl.pallas_call(
        functools.partial(_out_proj_kernel, seq=seq, tm=tm),
        grid=(m // tm, d // tn),
        in_specs=[
            pl.BlockSpec((tm, rw), lambda i, j: (i, 0)),
            pl.BlockSpec((tm, dw), lambda i, j: (i, 0)),
            pl.BlockSpec((tm, mw), lambda i, j: (i, 0)),
            pl.BlockSpec((rw, tn), lambda i, j: (0, j)),
            pl.BlockSpec((dw, tn), lambda i, j: (1, j)),
            pl.BlockSpec((mw, tn), lambda i, j: ((rw + dw) // mw, j)),
            pl.BlockSpec((tm, tn), lambda i, j: (i, j)),
            pl.BlockSpec((8, tn), lambda i, j: (0, gate_blk + j)),
        ],
        out_specs=pl.BlockSpec((tm, tn), lambda i, j: (i, j)),
        out_shape=jax.ShapeDtypeStruct((m, d), F32),
        compiler_params=_params(("arbitrary", "arbitrary")),
        name="out_proj",
    )(ret, dif, mlp, w_out, w_out, w_out, xx, mod)


def _final_norm_kernel(x_ref, w_ref, o_ref):
    x = x_ref[...]
    o_ref[...] = x * lax.rsqrt(jnp.mean(x * x, axis=-1, keepdims=True) + EPS) * w_ref[...]


def _final_norm(xx, w, seq):
    d = xx.shape[1]
    tr = 256
    return pl.pallas_call(
        _final_norm_kernel,
        grid=(seq // tr,),
        in_specs=[pl.BlockSpec((tr, d), lambda i: (i, 0)), pl.BlockSpec((1, d), lambda i: (0, 0))],
        out_specs=pl.BlockSpec((tr, d), lambda i: (i, 0)),
        out_shape=jax.ShapeDtypeStruct((seq, d), F32),
        compiler_params=_params(("arbitrary",)),
        name="final_norm",
    )(xx, w)


def _rope_tables(seq, ctx, dim, repeats):
    t = jnp.arange(seq, dtype=jnp.int32)
    r = (t // GRID_W).astype(F32)[:, None]
    c = (t % GRID_W).astype(F32)[:, None]
    n_freq = dim // 4
    inv = ROPE_BASE ** (-jnp.arange(n_freq, dtype=F32) / n_freq)
    ang = jnp.concatenate([r * inv, c * inv], axis=-1)
    cos = jnp.concatenate([jnp.cos(ang), jnp.cos(ang)], axis=-1)
    sin = jnp.concatenate([-jnp.sin(ang), jnp.sin(ang)], axis=-1)
    cos = jnp.concatenate([jnp.tile(cos, (1, repeats)), jnp.ones((ctx, repeats * dim), F32)], axis=0)
    sin = jnp.concatenate([jnp.tile(sin, (1, repeats)), jnp.zeros((ctx, repeats * dim), F32)], axis=0)
    return cos, sin


def kernel(x, c, ctx, c_ctx, w_in, w_out, w_ada, b_ada, norm_w, ret_decay_fwd, ret_decay_bwd,
           lam_q1, lam_k1, lam_q2, lam_k2, dif_norm_w, mlp_vnorm_w, mlp_ws, mlp_bs, final_norm_w):
    batch, seq, d = x.shape
    assert batch == 1, "the kernels process one sequence"
    n_ctx = ctx.shape[1]
    depth = w_in.shape[0]
    heads = ret_decay_fwd.shape[1]
    ret_w = heads * HEAD_DIM
    dif_w = ret_w
    mlp_w = mlp_vnorm_w.shape[1]
    assert 2 * ret_w + mlp_w == d and w_in.shape[2] == 8 * ret_w + 3 * mlp_w

    rk_blk, rv_blk, dk_blk, dv_blk = 0, heads, 2 * heads, 3 * heads
    rq_blk, dq_blk, rg_blk, dg_blk = 4 * heads, 5 * heads, 6 * heads, 7 * heads
    mu_blk = 8 * ret_w // mlp_w

    xx = jnp.concatenate([x[0], ctx[0]], axis=0)
    cc = jnp.zeros((8, d), F32).at[0].set(c[0]).at[1].set(c_ctx)
    mods = _modulation(cc, w_ada, b_ada)
    cos_r, sin_r = _rope_tables(seq, n_ctx, HEAD_DIM, 1)
    cos_d, sin_d = _rope_tables(seq, n_ctx, HEAD_DIM // 2, 2)
    w_in_b = w_in.astype(BF16)
    w_out_b = w_out.astype(BF16)

    for l in range(depth):
        lambda_init = 0.8 - 0.6 * math.exp(-0.3 * l)
        h = _norm_mod(xx, norm_w[l][None, :], mods[l], seq)
        z = _in_proj(h, w_in_b[l], BF16)
        ret = _retention(z, ret_decay_fwd, ret_decay_bwd, cos_r, sin_r, layer=l, heads=heads, seq=seq,
                         q_blk=rq_blk, k_blk=rk_blk, v_blk=rv_blk, g_blk=rg_blk)
        lams = [a[l][None, :] for a in (lam_q1, lam_k1, lam_q2, lam_k2)]
        dif = _attention(z, lams, cos_d, sin_d, dif_norm_w[l][None, :], lambda_init=lambda_init,
                         heads=heads, seq=seq, q_blk=dq_blk, k_blk=dk_blk, v_blk=dv_blk, g_blk=dg_blk)
        mlp = _chunk_mlp(z, mlp_vnorm_w[l][None, :], mlp_ws[l], mlp_bs[l].T, u_blk=mu_blk, width=mlp_w)
        xx = _out_proj(ret, dif, mlp, w_out_b[l], xx, mods[l], seq=seq)

    return _final_norm(xx, final_norm_w[None, :], seq)[None]
```

```python
import functools
import math

import jax
import jax.numpy as jnp
from jax import lax
from jax.experimental import pallas as pl
from jax.experimental.pallas import tpu as pltpu

HEAD_DIM = 128
CHUNK = 128
GRID_W = 64
ROPE_BASE = 10000.0
EPS = 1e-6
LOG2E = 1.4426950408889634
LANES = 128
NEG_BIG = -1e30
RET_UNROLL = 8
VMEM_LIMIT = 56 * 1024 * 1024

BF16 = jnp.bfloat16
F32 = jnp.float32

NT_DIMS = (((1,), (1,)), ((), ()))
TN_DIMS = (((0,), (0,)), ((), ()))


def _params(semantics):
    return pltpu.CompilerParams(dimension_semantics=semantics, vmem_limit_bytes=VMEM_LIMIT)


def _pick(n, candidates):
    for c in candidates:
        if n % c == 0:
            return c
    return n


def _silu(x):
    return x * (1.0 / (1.0 + jnp.exp(-x)))


def _gelu(x):
    return 0.5 * x * (1.0 + lax.erf(x * (2.0 ** -0.5)))


def _mod_kernel(cc_ref, w_ref, b_ref, o_ref):
    a = _silu(cc_ref[...]).astype(BF16)
    o_ref[...] = jnp.dot(a, w_ref[...].astype(BF16), preferred_element_type=F32) + b_ref[...]


def _modulation(cc, w_ada, b_ada):
    depth, d, n3 = w_ada.shape
    tn = _pick(n3, (512, 256, 128))
    return pl.pallas_call(
        _mod_kernel,
        grid=(depth, n3 // tn),
        in_specs=[
            pl.BlockSpec((8, d), lambda l, j: (0, 0)),
            pl.BlockSpec((None, d, tn), lambda l, j: (l, 0, j)),
            pl.BlockSpec((None, 1, tn), lambda l, j: (l, 0, j)),
        ],
        out_specs=pl.BlockSpec((None, 8, tn), lambda l, j: (l, 0, j)),
        out_shape=jax.ShapeDtypeStruct((depth, 8, n3), F32),
        compiler_params=_params(("arbitrary", "arbitrary")),
        name="modulation",
    )(cc, w_ada, b_ada.reshape(depth, 1, n3))


def _norm_mod_kernel(x_ref, nw_ref, mod_ref, h_ref, *, d, n_latent_blocks):
    x = x_ref[...]
    y = x * lax.rsqrt(jnp.mean(x * x, axis=-1, keepdims=True) + EPS) * nw_ref[...]
    is_ctx = pl.program_id(0) >= n_latent_blocks
    shift = jnp.where(is_ctx, mod_ref[1:2, 0:d], mod_ref[0:1, 0:d])
    scale = jnp.where(is_ctx, mod_ref[1:2, d:2 * d], mod_ref[0:1, d:2 * d])
    h_ref[...] = (y * (1.0 + scale) + shift).astype(h_ref.dtype)


def _norm_mod(xx, nw, mod, seq):
    m, d = xx.shape
    tr = 256
    return pl.pallas_call(
        functools.partial(_norm_mod_kernel, d=d, n_latent_blocks=seq // tr),
        grid=(m // tr,),
        in_specs=[
            pl.BlockSpec((tr, d), lambda i: (i, 0)),
            pl.BlockSpec((1, d), lambda i: (0, 0)),
            pl.BlockSpec((8, 3 * d), lambda i: (0, 0)),
        ],
        out_specs=pl.BlockSpec((tr, d), lambda i: (i, 0)),
        out_shape=jax.ShapeDtypeStruct((m, d), BF16),
        compiler_params=_params(("arbitrary",)),
        name="norm_mod",
    )(xx, nw, mod)


def _matmul_kernel(a_ref, b_ref, o_ref):
    o_ref[...] = jnp.dot(a_ref[...], b_ref[...].astype(BF16), preferred_element_type=F32).astype(o_ref.dtype)


def _in_proj(h, w, layer, out_dtype):
    m, k = h.shape
    n = w.shape[2]
    tm = _pick(m, (1056, 768, 640, 256))
    tn = _pick(n, (512, 256))
    return pl.pallas_call(
        _matmul_kernel,
        grid=(m // tm, n // tn),
        in_specs=[
            pl.BlockSpec((tm, k), lambda i, j: (i, 0)),
            pl.BlockSpec((None, k, tn), lambda i, j: (layer, 0, j)),
        ],
        out_specs=pl.BlockSpec((tm, tn), lambda i, j: (i, j)),
        out_shape=jax.ShapeDtypeStruct((m, n), out_dtype),
        compiler_params=_params(("arbitrary", "arbitrary")),
        name="in_proj",
    )(h, w)


def _retention_kernel(decf_ref, decb_ref, q_ref, k_ref, v_ref, g_ref, cos_ref, sin_ref, o_ref,
                      ob_ref, *, layer, n_latent_chunks, n_ctx_chunks):
    head = pl.program_id(0)
    lgf = jnp.log1p(-jnp.exp(jnp.full((1, LANES), decf_ref[layer, head], F32)))
    lgb = jnp.log1p(-jnp.exp(jnp.full((1, LANES), decb_ref[layer, head], F32)))
    ri = lax.broadcasted_iota(jnp.int32, (CHUNK, CHUNK), 0)
    ci = lax.broadcasted_iota(jnp.int32, (CHUNK, CHUNK), 1)
    rel = (ri - ci).astype(F32)
    pos = ri.astype(F32)
    dmat = jnp.where(rel >= 0, jnp.exp(lgf * jnp.maximum(rel, 0.0)), jnp.exp(lgb * jnp.maximum(-rel, 0.0)))
    wfk = jnp.exp(lgf * (CHUNK - 1.0 - pos))
    wbk = jnp.exp(lgb * pos)
    dqf = jnp.exp(lgf * (pos + 1.0))
    dqb = jnp.exp(lgb * (CHUNK - pos))
    cdf = jnp.exp(lgf * float(CHUNK))
    cdb = jnp.exp(lgb * float(CHUNK))
    ret_scale = HEAD_DIM ** -0.5

    def rows_of(c):
        return pl.ds(pl.multiple_of(c * CHUNK, CHUNK), CHUNK)

    def roped(rows):
        cos = cos_ref[rows, :]
        sin = sin_ref[rows, :]
        q = q_ref[rows, :].astype(F32)
        k = k_ref[rows, :].astype(F32)
        qr = q * cos + pltpu.roll(q, HEAD_DIM // 2, 1) * sin
        kr = (k * cos + pltpu.roll(k, HEAD_DIM // 2, 1) * sin) * ret_scale
        return qr, kr

    def run_sequence(first, n, sf, sb):
        def bwd(t, sb):
            rows = rows_of(first + n - 1 - t)
            qr, kr = roped(rows)
            vb = v_ref[rows, :].astype(BF16)
            ob_ref[rows, :] = jnp.dot((qr * dqb).astype(BF16), sb.astype(BF16), preferred_element_type=F32)
            kv = lax.dot_general((kr * wbk).astype(BF16), vb, TN_DIMS, preferred_element_type=F32)
            return sb * cdb + kv

        sb = lax.fori_loop(0, n, bwd, sb, unroll=min(n, RET_UNROLL))

        def fwd(t, sf):
            rows = rows_of(first + t)
            qr, kr = roped(rows)
            vb = v_ref[rows, :].astype(BF16)
            a = lax.dot_general(qr.astype(BF16), kr.astype(BF16), NT_DIMS, preferred_element_type=F32) * dmat
            o = jnp.dot(a.astype(BF16), vb, preferred_element_type=F32)
            o = o + jnp.dot((qr * dqf).astype(BF16), sf.astype(BF16), preferred_element_type=F32)
            o = o + ob_ref[rows, :]
            o = o * lax.rsqrt(jnp.mean(o * o, axis=-1, keepdims=True) + EPS)
            o_ref[rows, :] = (o * _silu(g_ref[rows, :].astype(F32))).astype(o_ref.dtype)
            kv = lax.dot_general((kr * wfk).astype(BF16), vb, TN_DIMS, preferred_element_type=F32)
            return sf * cdf + kv

        sf = lax.fori_loop(0, n, fwd, sf, unroll=min(n, RET_UNROLL))
        return sf, sb

    zero = jnp.zeros((HEAD_DIM, HEAD_DIM), F32)
    sf, sb = run_sequence(n_latent_chunks, n_ctx_chunks, zero, zero)
    run_sequence(0, n_latent_chunks, sf, sb)


def _retention(z, dec_f, dec_b, cos, sin, *, layer, heads, seq, q_blk, k_blk, v_blk, g_blk):
    m = z.shape[0]
    col = lambda base: pl.BlockSpec((m, HEAD_DIM), lambda h: (0, base + h))
    smem = pl.BlockSpec(memory_space=pltpu.SMEM)
    full = pl.BlockSpec((m, HEAD_DIM), lambda h: (0, 0), pipeline_mode=pl.Buffered(1))
    return pl.pallas_call(
        functools.partial(_retention_kernel, layer=layer, n_latent_chunks=seq // CHUNK,
                          n_ctx_chunks=(m - seq) // CHUNK),
        grid=(heads,),
        in_specs=[smem, smem, col(q_blk), col(k_blk), col(v_blk), col(g_blk), full, full],
        out_specs=pl.BlockSpec((m, HEAD_DIM), lambda h: (0, h)),
        out_shape=jax.ShapeDtypeStruct((m, heads * HEAD_DIM), BF16),
        scratch_shapes=[pltpu.VMEM((m, HEAD_DIM), F32)],
        compiler_params=_params(("arbitrary",)),
        name="retention",
    )(dec_f, dec_b, z, z, z, z, cos, sin)


def _swap_half_pairs(x):
    lane = lax.broadcasted_iota(jnp.int32, x.shape, 1)
    return jnp.where(lane % 64 < 32, pltpu.roll(x, 96, 1), pltpu.roll(x, 32, 1))


def _dif_rope(x, cos, sin):
    return x * cos + _swap_half_pairs(x) * sin


def _attention_kernel(lq1_ref, lk1_ref, lq2_ref, lk2_ref, q_ref, cosq_ref, sinq_ref, k_ref, v_ref,
                      cosk_ref, sink_ref, g_ref, nw_ref, o_ref, ks_ref, vtx_ref, vtc_ref, s_ref, sc_ref,
                      *, lambda_init, seq, tq, tk):
    qi = pl.program_id(1)
    m = k_ref.shape[0]
    ctx = m - seq
    vrows = vtc_ref.shape[0]
    ones_rows = (lax.broadcasted_iota(jnp.int32, (vrows - HEAD_DIM, 1), 0) == 0).astype(BF16)

    @pl.when(qi == 0)
    def _():
        def prep_rows(rows, n):
            k = _dif_rope(k_ref[rows, :].astype(F32), cosk_ref[rows, :], sink_ref[rows, :])
            ks_ref[rows, :] = k.astype(BF16)
            vt = v_ref[rows, :].astype(F32).T.astype(BF16)
            return jnp.concatenate([vt, jnp.broadcast_to(ones_rows, (vrows - HEAD_DIM, n))], axis=0)

        def prep(c, carry):
            vtx_ref[c] = prep_rows(pl.ds(pl.multiple_of(c * tk, tk), tk), tk)
            return carry
        lax.fori_loop(0, seq // tk, prep, 0)
        vtc_ref[...] = prep_rows(pl.ds(seq, ctx), ctx)

    q = _dif_rope(q_ref[...].astype(F32), cosq_ref[...], sinq_ref[...]) * ((HEAD_DIM // 2) ** -0.5 * LOG2E)
    lane = lax.broadcasted_iota(jnp.int32, q.shape, 1)
    q0 = jnp.where(lane < HEAD_DIM // 2, q, 0.0).astype(BF16)
    q1 = jnp.where(lane >= HEAD_DIM // 2, q, 0.0).astype(BF16)

    def scores(rows, dst):
        kb = ks_ref[rows, :]
        cmax = []
        for comp, qm in enumerate((q0, q1)):
            s = lax.dot_general(kb, qm, NT_DIMS, preferred_element_type=F32)
            dst[comp] = s
            cmax.append(jnp.max(s, axis=0, keepdims=True))
        return tuple(cmax)

    def consume(src, vt, cmax, state):
        new = []
        for comp in range(2):
            mx, acc = state[comp]
            mn = jnp.maximum(mx, cmax[comp])
            p = jnp.exp2(src[comp] - mn).astype(BF16)
            acc = jnp.exp2(mx - mn) * acc + jnp.dot(vt, p, preferred_element_type=F32)
            new.append((mn, acc))
        return tuple(new)

    def finalize(state):
        (_, acc0), (_, acc1) = state
        lam = (jnp.exp(jnp.sum(lq1_ref[...] * lk1_ref[...], axis=-1, keepdims=True))
               - jnp.exp(jnp.sum(lq2_ref[...] * lk2_ref[...], axis=-1, keepdims=True)) + lambda_init)
        l0 = acc0[HEAD_DIM:HEAD_DIM + 1, :]
        l1 = acc1[HEAD_DIM:HEAD_DIM + 1, :]
        ot = acc0[:HEAD_DIM, :] * (1.0 / l0) - lam * (acc1[:HEAD_DIM, :] * (1.0 / l1))
        o = ot.T
        o = o * lax.rsqrt(jnp.mean(o * o, axis=-1, keepdims=True) + EPS) * nw_ref[...] * (1.0 - lambda_init)
        o_ref[...] = (o * _silu(g_ref[...].astype(F32))).astype(o_ref.dtype)

    init = ((jnp.full((1, tq), NEG_BIG, F32), jnp.zeros((vrows, tq), F32)),) * 2
    ctx_rows = pl.ds(seq, ctx)
    n_chunks = seq // tk

    @pl.when(qi < seq // tq)
    def _():
        cmax = scores(pl.ds(0, tk), s_ref.at[0])
        state = init
        for c in range(n_chunks):
            if c + 1 < n_chunks:
                nxt = scores(pl.ds((c + 1) * tk, tk), s_ref.at[(c + 1) % 2])
            else:
                nxt = scores(ctx_rows, sc_ref)
            state = consume(s_ref.at[c % 2], vtx_ref[c], cmax, state)
            cmax = nxt
        finalize(consume(sc_ref, vtc_ref[...], cmax, state))

    @pl.when(qi >= seq // tq)
    def _():
        cmax = scores(ctx_rows, sc_ref)
        finalize(consume(sc_ref, vtc_ref[...], cmax, init))


def _attention(z, lams, cos, sin, nw, *, lambda_init, heads, seq, q_blk, k_blk, v_blk, g_blk):
    m = z.shape[0]
    ctx = m - seq
    tq = 256
    tk = _pick(seq, (2048, 1024, 512, 256, 128))
    assert seq % tq == 0 and ctx == tq, "one query block must hold exactly the context rows"
    vrows = HEAD_DIM + 16
    lam_spec = pl.BlockSpec((1, HEAD_DIM // 2), lambda h, i: (0, 0))
    qrow = lambda base: pl.BlockSpec((tq, HEAD_DIM), lambda h, i: (i, base + h))
    kcol = lambda base: pl.BlockSpec((m, HEAD_DIM), lambda h, i: (0, base + h))
    table = pl.BlockSpec((m, HEAD_DIM), lambda h, i: (0, 0), pipeline_mode=pl.Buffered(1))
    return pl.pallas_call(
        functools.partial(_attention_kernel, lambda_init=lambda_init, seq=seq, tq=tq, tk=tk),
        grid=(heads, m // tq),
        in_specs=[lam_spec, lam_spec, lam_spec, lam_spec,
                  qrow(q_blk),
                  pl.BlockSpec((tq, HEAD_DIM), lambda h, i: (i, 0)),
                  pl.BlockSpec((tq, HEAD_DIM), lambda h, i: (i, 0)),
                  kcol(k_blk), kcol(v_blk), table, table,
                  qrow(g_blk),
                  pl.BlockSpec((1, HEAD_DIM), lambda h, i: (0, 0))],
        out_specs=pl.BlockSpec((tq, HEAD_DIM), lambda h, i: (i, h)),
        out_shape=jax.ShapeDtypeStruct((m, heads * HEAD_DIM), BF16),
        scratch_shapes=[pltpu.VMEM((m, HEAD_DIM), BF16),
                        pltpu.VMEM((seq // tk, vrows, tk), BF16),
                        pltpu.VMEM((vrows, ctx), BF16),
                        pltpu.VMEM((2, 2, tk, tq), F32),
                        pltpu.VMEM((2, ctx, tq), F32)],
        compiler_params=_params(("arbitrary", "arbitrary")),
        name="diff_attention",
    )(*lams, z, cos, sin, z, z, cos, sin, z, nw)


def _mlp_kernel(u_ref, v_ref, g_ref, vnw_ref, ws_ref, bst_ref, o_ref, *, groups):
    tr, width = u_ref.shape
    gd = width // groups
    v = _gelu(v_ref[...].astype(F32))
    v = (v * lax.rsqrt(jnp.mean(v * v, axis=-1, keepdims=True) + EPS) * vnw_ref[...]).astype(BF16)
    for c in range(tr // CHUNK):
        rows = slice(c * CHUNK, (c + 1) * CHUNK)
        for g in range(groups):
            cols = slice(g * gd, (g + 1) * gd)
            mixed = jnp.dot(ws_ref[g].astype(BF16), v[rows, cols], preferred_element_type=F32) + bst_ref[:, g:g + 1]
            u = _gelu(u_ref[rows, cols].astype(F32))
            o_ref[rows, cols] = (u * mixed * _silu(g_ref[rows, cols].astype(F32))).astype(o_ref.dtype)


def _chunk_mlp(z, vnw, ws, bst, *, u_blk, width):
    m = z.shape[0]
    groups = ws.shape[0]
    tr = 256
    col = lambda b: pl.BlockSpec((tr, width), lambda i: (i, b))
    return pl.pallas_call(
        functools.partial(_mlp_kernel, groups=groups),
        grid=(m // tr,),
        in_specs=[col(u_blk), col(u_blk + 1), col(u_blk + 2),
                  pl.BlockSpec((1, width), lambda i: (0, 0)),
                  pl.BlockSpec((groups, CHUNK, CHUNK), lambda i: (0, 0, 0)),
                  pl.BlockSpec((CHUNK, groups), lambda i: (0, 0))],
        out_specs=pl.BlockSpec((tr, width), lambda i: (i, 0)),
        out_shape=jax.ShapeDtypeStruct((m, width), BF16),
        compiler_params=_params(("arbitrary",)),
        name="chunk_mlp",
    )(z, z, z, vnw, ws, bst)


def _out_proj_kernel(ret_ref, dif_ref, mlp_ref, wr_ref, wd_ref, wm_ref, x_ref, gate_ref, o_ref, *, seq, tm):
    y = jnp.dot(ret_ref[...], wr_ref[...].astype(BF16), preferred_element_type=F32)
    y = y + jnp.dot(dif_ref[...], wd_ref[...].astype(BF16), preferred_element_type=F32)
    y = y + jnp.dot(mlp_ref[...], wm_ref[...].astype(BF16), preferred_element_type=F32)
    row = pl.program_id(0) * tm + lax.broadcasted_iota(jnp.int32, y.shape, 0)
    gate = jnp.where(row >= seq, gate_ref[1:2, :], gate_ref[0:1, :])
    o_ref[...] = x_ref[...] + gate * y


def _out_proj(ret, dif, mlp, w_out, layer, xx, mod, *, seq):
    m, d = xx.shape
    rw, dw, mw = ret.shape[1], dif.shape[1], mlp.shape[1]
    assert rw == dw and (rw + dw) % mw == 0
    tm = _pick(m, (1056, 768, 640, 256))
    tn = _pick(d, (512, 256))
    gate_blk = 2 * d // tn
    return pl.pallas_call(
        functools.partial(_out_proj_kernel, seq=seq, tm=tm),
        grid=(m // tm, d // tn),
        in_specs=[
            pl.BlockSpec((tm, rw), lambda i, j: (i, 0)),
            pl.BlockSpec((tm, dw), lambda i, j: (i, 0)),
            pl.BlockSpec((tm, mw), lambda i, j: (i, 0)),
            pl.BlockSpec((None, rw, tn), lambda i, j: (layer, 0, j)),
            pl.BlockSpec((None, dw, tn), lambda i, j: (layer, 1, j)),
            pl.BlockSpec((None, mw, tn), lambda i, j: (layer, (rw + dw) // mw, j)),
            pl.BlockSpec((tm, tn), lambda i, j: (i, j)),
            pl.BlockSpec((8, tn), lambda i, j: (0, gate_blk + j)),
        ],
        out_specs=pl.BlockSpec((tm, tn), lambda i, j: (i, j)),
        out_shape=jax.ShapeDtypeStruct((m, d), F32),
        compiler_params=_params(("arbitrary", "arbitrary")),
        name="out_proj",
    )(ret, dif, mlp, w_out, w_out, w_out, xx, mod)


def _final_norm_kernel(x_ref, w_ref, o_ref):
    x = x_ref[...]
    o_ref[...] = x * lax.rsqrt(jnp.mean(x * x, axis=-1, keepdims=True) + EPS) * w_ref[...]


def _final_norm(xx, w, seq):
    d = xx.shape[1]
    tr = 256
    return pl.pallas_call(
        _final_norm_kernel,
        grid=(seq // tr,),
        in_specs=[pl.BlockSpec((tr, d), lambda i: (i, 0)), pl.BlockSpec((1, d), lambda i: (0, 0))],
        out_specs=pl.BlockSpec((tr, d), lambda i: (i, 0)),
        out_shape=jax.ShapeDtypeStruct((seq, d), F32),
        compiler_params=_params(("arbitrary",)),
        name="final_norm",
    )(xx, w)


def _rope_tables(seq, ctx, dim, repeats):
    t = jnp.arange(seq, dtype=jnp.int32)
    r = (t // GRID_W).astype(F32)[:, None]
    c = (t % GRID_W).astype(F32)[:, None]
    n_freq = dim // 4
    inv = ROPE_BASE ** (-jnp.arange(n_freq, dtype=F32) / n_freq)
    ang = jnp.concatenate([r * inv, c * inv], axis=-1)
    cos = jnp.concatenate([jnp.cos(ang), jnp.cos(ang)], axis=-1)
    sin = jnp.concatenate([-jnp.sin(ang), jnp.sin(ang)], axis=-1)
    cos = jnp.concatenate([jnp.tile(cos, (1, repeats)), jnp.ones((ctx, repeats * dim), F32)], axis=0)
    sin = jnp.concatenate([jnp.tile(sin, (1, repeats)), jnp.zeros((ctx, repeats * dim), F32)], axis=0)
    return cos, sin


def kernel(x, c, ctx, c_ctx, w_in, w_out, w_ada, b_ada, norm_w, ret_decay_fwd, ret_decay_bwd,
           lam_q1, lam_k1, lam_q2, lam_k2, dif_norm_w, mlp_vnorm_w, mlp_ws, mlp_bs, final_norm_w):
    batch, seq, d = x.shape
    assert batch == 1, "the kernels process one sequence"
    n_ctx = ctx.shape[1]
    depth = w_in.shape[0]
    heads = ret_decay_fwd.shape[1]
    ret_w = heads * HEAD_DIM
    dif_w = ret_w
    mlp_w = mlp_vnorm_w.shape[1]
    assert 2 * ret_w + mlp_w == d and w_in.shape[2] == 8 * ret_w + 3 * mlp_w

    rk_blk, rv_blk, dk_blk, dv_blk = 0, heads, 2 * heads, 3 * heads
    rq_blk, dq_blk, rg_blk, dg_blk = 4 * heads, 5 * heads, 6 * heads, 7 * heads
    mu_blk = 8 * ret_w // mlp_w

    xx = jnp.concatenate([x[0], ctx[0]], axis=0)
    cc = jnp.zeros((8, d), F32).at[0].set(c[0]).at[1].set(c_ctx)
    mods = _modulation(cc, w_ada, b_ada)
    cos_r, sin_r = _rope_tables(seq, n_ctx, HEAD_DIM, 1)
    cos_d, sin_d = _rope_tables(seq, n_ctx, HEAD_DIM // 2, 2)

    for l in range(depth):
        lambda_init = 0.8 - 0.6 * math.exp(-0.3 * l)
        h = _norm_mod(xx, norm_w[l][None, :], mods[l], seq)
        z = _in_proj(h, w_in, l, BF16)
        ret = _retention(z, ret_decay_fwd, ret_decay_bwd, cos_r, sin_r, layer=l, heads=heads, seq=seq,
                         q_blk=rq_blk, k_blk=rk_blk, v_blk=rv_blk, g_blk=rg_blk)
        lams = [a[l][None, :] for a in (lam_q1, lam_k1, lam_q2, lam_k2)]
        dif = _attention(z, lams, cos_d, sin_d, dif_norm_w[l][None, :], lambda_init=lambda_init,
                         heads=heads, seq=seq, q_blk=dq_blk, k_blk=dk_blk, v_blk=dv_blk, g_blk=dg_blk)
        mlp = _chunk_mlp(z, mlp_vnorm_w[l][None, :], mlp_ws[l], mlp_bs[l].T, u_blk=mu_blk, width=mlp_w)
        xx = _out_proj(ret, dif, mlp, w_out, l, xx, mods[l], seq=seq)

    return _final_norm(xx, final_norm_w[None, :], seq)[None]
```

```python
import functools
import math

import jax
import jax.numpy as jnp
from jax import lax
from jax.experimental import pallas as pl
from jax.experimental.pallas import tpu as pltpu

HEAD_DIM = 128
CHUNK = 128
GRID_W = 64
ROPE_BASE = 10000.0
EPS = 1e-6
LOG2E = 1.4426950408889634
LANES = 128
NEG_BIG = -1e30
RET_UNROLL = 8
VMEM_LIMIT = 56 * 1024 * 1024

BF16 = jnp.bfloat16
F32 = jnp.float32

NT_DIMS = (((1,), (1,)), ((), ()))
TN_DIMS = (((0,), (0,)), ((), ()))


def _params(semantics, **kwargs):
    return pltpu.CompilerParams(dimension_semantics=semantics, vmem_limit_bytes=VMEM_LIMIT, **kwargs)


def _pick(n, candidates):
    for c in candidates:
        if n % c == 0:
            return c
    return n


def _silu(x):
    return x * (1.0 / (1.0 + jnp.exp(-x)))


def _gelu(x):
    return 0.5 * x * (1.0 + lax.erf(x * (2.0 ** -0.5)))


def _mod_kernel(cc_ref, w_ref, b_ref, o_ref):
    a = _silu(cc_ref[...]).astype(BF16)
    o_ref[...] = jnp.dot(a, w_ref[...].astype(BF16), preferred_element_type=F32) + b_ref[...]


def _modulation(cc, w_ada, b_ada):
    depth, d, n3 = w_ada.shape
    tn = _pick(n3, (512, 256, 128))
    return pl.pallas_call(
        _mod_kernel,
        grid=(depth, n3 // tn),
        in_specs=[
            pl.BlockSpec((8, d), lambda l, j: (0, 0)),
            pl.BlockSpec((None, d, tn), lambda l, j: (l, 0, j)),
            pl.BlockSpec((None, 1, tn), lambda l, j: (l, 0, j)),
        ],
        out_specs=pl.BlockSpec((None, 8, tn), lambda l, j: (l, 0, j)),
        out_shape=jax.ShapeDtypeStruct((depth, 8, n3), F32),
        compiler_params=_params(("arbitrary", "arbitrary")),
        name="modulation",
    )(cc, w_ada, b_ada.reshape(depth, 1, n3))


def _norm_mod_kernel(x_ref, nw_ref, mod_ref, h_ref, *, d, n_latent_blocks):
    x = x_ref[...]
    y = x * lax.rsqrt(jnp.mean(x * x, axis=-1, keepdims=True) + EPS) * nw_ref[...]
    is_ctx = pl.program_id(0) >= n_latent_blocks
    shift = jnp.where(is_ctx, mod_ref[1:2, 0:d], mod_ref[0:1, 0:d])
    scale = jnp.where(is_ctx, mod_ref[1:2, d:2 * d], mod_ref[0:1, d:2 * d])
    h_ref[...] = (y * (1.0 + scale) + shift).astype(h_ref.dtype)


def _norm_mod(xx, nw, mod, seq):
    m, d = xx.shape
    tr = 256
    return pl.pallas_call(
        functools.partial(_norm_mod_kernel, d=d, n_latent_blocks=seq // tr),
        grid=(m // tr,),
        in_specs=[
            pl.BlockSpec((tr, d), lambda i: (i, 0)),
            pl.BlockSpec((1, d), lambda i: (0, 0)),
            pl.BlockSpec((8, 3 * d), lambda i: (0, 0)),
        ],
        out_specs=pl.BlockSpec((tr, d), lambda i: (i, 0)),
        out_shape=jax.ShapeDtypeStruct((m, d), BF16),
        compiler_params=_params(("arbitrary",)),
        name="norm_mod",
    )(xx, nw, mod)


def _matmul_kernel(a_ref, b_ref, o_ref):
    o_ref[...] = jnp.dot(a_ref[...], b_ref[...].astype(BF16), preferred_element_type=F32).astype(o_ref.dtype)


def _in_proj(h, w, layer, out_dtype):
    m, k = h.shape
    n = w.shape[2]
    tm = _pick(m, (1056, 768, 640, 256))
    tn = _pick(n, (512, 256))
    return pl.pallas_call(
        _matmul_kernel,
        grid=(m // tm, n // tn),
        in_specs=[
            pl.BlockSpec((tm, k), lambda i, j: (i, 0)),
            pl.BlockSpec((None, k, tn), lambda i, j: (layer, 0, j)),
        ],
        out_specs=pl.BlockSpec((tm, tn), lambda i, j: (i, j)),
        out_shape=jax.ShapeDtypeStruct((m, n), out_dtype),
        compiler_params=_params(("arbitrary", "arbitrary")),
        name="in_proj",
    )(h, w)


def _retention_kernel(decf_ref, decb_ref, q_ref, k_ref, v_ref, g_ref, cos_ref, sin_ref, o_ref,
                      ob_ref, *, layer, n_latent_chunks, n_ctx_chunks):
    head = pl.program_id(0)
    lgf = jnp.log1p(-jnp.exp(jnp.full((1, LANES), decf_ref[layer, head], F32)))
    lgb = jnp.log1p(-jnp.exp(jnp.full((1, LANES), decb_ref[layer, head], F32)))
    ri = lax.broadcasted_iota(jnp.int32, (CHUNK, CHUNK), 0)
    ci = lax.broadcasted_iota(jnp.int32, (CHUNK, CHUNK), 1)
    rel = (ri - ci).astype(F32)
    pos = ri.astype(F32)
    dmat = jnp.where(rel >= 0, jnp.exp(lgf * jnp.maximum(rel, 0.0)), jnp.exp(lgb * jnp.maximum(-rel, 0.0)))
    wfk = jnp.exp(lgf * (CHUNK - 1.0 - pos))
    wbk = jnp.exp(lgb * pos)
    dqf = jnp.exp(lgf * (pos + 1.0))
    dqb = jnp.exp(lgb * (CHUNK - pos))
    cdf = jnp.exp(lgf * float(CHUNK))
    cdb = jnp.exp(lgb * float(CHUNK))
    ret_scale = HEAD_DIM ** -0.5

    def rows_of(c):
        return pl.ds(pl.multiple_of(c * CHUNK, CHUNK), CHUNK)

    def roped(rows):
        cos = cos_ref[rows, :]
        sin = sin_ref[rows, :]
        q = q_ref[rows, :].astype(F32)
        k = k_ref[rows, :].astype(F32)
        qr = q * cos + pltpu.roll(q, HEAD_DIM // 2, 1) * sin
        kr = (k * cos + pltpu.roll(k, HEAD_DIM // 2, 1) * sin) * ret_scale
        return qr, kr

    def run_sequence(first, n, sf, sb):
        def bwd(t, sb):
            rows = rows_of(first + n - 1 - t)
            qr, kr = roped(rows)
            vb = v_ref[rows, :].astype(BF16)
            ob_ref[rows, :] = jnp.dot((qr * dqb).astype(BF16), sb.astype(BF16), preferred_element_type=F32)
            kv = lax.dot_general((kr * wbk).astype(BF16), vb, TN_DIMS, preferred_element_type=F32)
            return sb * cdb + kv

        sb = lax.fori_loop(0, n, bwd, sb, unroll=min(n, RET_UNROLL))

        def fwd(t, sf):
            rows = rows_of(first + t)
            qr, kr = roped(rows)
            vb = v_ref[rows, :].astype(BF16)
            a = lax.dot_general(qr.astype(BF16), kr.astype(BF16), NT_DIMS, preferred_element_type=F32) * dmat
            o = jnp.dot(a.astype(BF16), vb, preferred_element_type=F32)
            o = o + jnp.dot((qr * dqf).astype(BF16), sf.astype(BF16), preferred_element_type=F32)
            o = o + ob_ref[rows, :]
            o = o * lax.rsqrt(jnp.mean(o * o, axis=-1, keepdims=True) + EPS)
            o_ref[rows, :] = (o * _silu(g_ref[rows, :].astype(F32))).astype(o_ref.dtype)
            kv = lax.dot_general((kr * wfk).astype(BF16), vb, TN_DIMS, preferred_element_type=F32)
            return sf * cdf + kv

        sf = lax.fori_loop(0, n, fwd, sf, unroll=min(n, RET_UNROLL))
        return sf, sb

    zero = jnp.zeros((HEAD_DIM, HEAD_DIM), F32)
    sf, sb = run_sequence(n_latent_chunks, n_ctx_chunks, zero, zero)
    run_sequence(0, n_latent_chunks, sf, sb)


def _retention(z, dec_f, dec_b, cos, sin, *, layer, heads, seq, q_blk, k_blk, v_blk, g_blk):
    m = z.shape[0]
    col = lambda base: pl.BlockSpec((m, HEAD_DIM), lambda h: (0, base + h))
    smem = pl.BlockSpec(memory_space=pltpu.SMEM)
    full = pl.BlockSpec((m, HEAD_DIM), lambda h: (0, 0), pipeline_mode=pl.Buffered(1))
    return pl.pallas_call(
        functools.partial(_retention_kernel, layer=layer, n_latent_chunks=seq // CHUNK,
                          n_ctx_chunks=(m - seq) // CHUNK),
        grid=(heads,),
        in_specs=[smem, smem, col(q_blk), col(k_blk), col(v_blk), col(g_blk), full, full],
        out_specs=pl.BlockSpec((m, HEAD_DIM), lambda h: (0, h)),
        out_shape=jax.ShapeDtypeStruct((m, heads * HEAD_DIM), BF16),
        scratch_shapes=[pltpu.VMEM((m, HEAD_DIM), F32)],
        compiler_params=_params(("arbitrary",)),
        name="retention",
    )(dec_f, dec_b, z, z, z, z, cos, sin)


def _swap_half_pairs(x):
    lane = lax.broadcasted_iota(jnp.int32, x.shape, 1)
    return jnp.where(lane % 64 < 32, pltpu.roll(x, 96, 1), pltpu.roll(x, 32, 1))


def _dif_rope(x, cos, sin):
    return x * cos + _swap_half_pairs(x) * sin


def _attention_kernel(lq1_ref, lk1_ref, lq2_ref, lk2_ref, q_ref, cosq_ref, sinq_ref, qn_ref, cosn_ref, sinn_ref,
                      k_ref, v_ref, cosk_ref, sink_ref, g_ref, nw_ref, o_ref,
                      ks_ref, vtx_ref, vtl_ref, s_ref, cm_ref, *, lambda_init, seq, tq, tk):
    qi = pl.program_id(1)
    m = k_ref.shape[0]
    ctx = m - seq
    vrows = vtl_ref.shape[0]
    n_chunks = seq // tk
    last = n_chunks - 1
    ones_rows = (lax.broadcasted_iota(jnp.int32, (vrows - HEAD_DIM, 1), 0) == 0).astype(BF16)

    def chunk_len(c):
        return tk + ctx if c == last else tk

    @pl.when(qi == 0)
    def _():
        def prep_rows(rows, n):
            k = _dif_rope(k_ref[rows, :].astype(F32), cosk_ref[rows, :], sink_ref[rows, :])
            ks_ref[rows, :] = k.astype(BF16)
            vt = v_ref[rows, :].astype(F32).T.astype(BF16)
            return jnp.concatenate([vt, jnp.broadcast_to(ones_rows, (vrows - HEAD_DIM, n))], axis=0)

        def prep(c, carry):
            vtx_ref[c] = prep_rows(pl.ds(pl.multiple_of(c * tk, tk), tk), tk)
            return carry
        lax.fori_loop(0, last, prep, 0)
        vtl_ref[:, 0:tk] = prep_rows(pl.ds(last * tk, tk), tk)
        vtl_ref[:, tk:tk + ctx] = prep_rows(pl.ds(seq, ctx), ctx)

    def query_operands(ref, cos_ref, sin_ref):
        q = _dif_rope(ref[...].astype(F32), cos_ref[...], sin_ref[...]) * ((HEAD_DIM // 2) ** -0.5 * LOG2E)
        lane = lax.broadcasted_iota(jnp.int32, q.shape, 1)
        q0 = jnp.where(lane < HEAD_DIM // 2, q, 0.0).astype(BF16)
        q1 = jnp.where(lane >= HEAD_DIM // 2, q, 0.0).astype(BF16)
        return q0, q1

    def scores(qs, row0, n, slot):
        kb = ks_ref[pl.ds(row0, n), :]
        cmax = []
        for comp, qm in enumerate(qs):
            s = lax.dot_general(kb, qm, NT_DIMS, preferred_element_type=F32)
            s_ref[slot, comp, 0:n, :] = s
            cmax.append(jnp.max(s, axis=0, keepdims=True))
        return tuple(cmax)

    def consume(slot, n, vt, cmax, state):
        new = []
        for comp in range(2):
            mx, acc = state[comp]
            mn = jnp.maximum(mx, cmax[comp])
            p = jnp.exp2(s_ref[slot, comp, 0:n, :] - mn).astype(BF16)
            acc = jnp.exp2(mx - mn) * acc + jnp.dot(vt, p, preferred_element_type=F32)
            new.append((mn, acc))
        return tuple(new)

    def finalize(state):
        (_, acc0), (_, acc1) = state
        lam = (jnp.exp(jnp.sum(lq1_ref[...] * lk1_ref[...], axis=-1, keepdims=True))
               - jnp.exp(jnp.sum(lq2_ref[...] * lk2_ref[...], axis=-1, keepdims=True)) + lambda_init)
        l0 = acc0[HEAD_DIM:HEAD_DIM + 1, :]
        l1 = acc1[HEAD_DIM:HEAD_DIM + 1, :]
        ot = acc0[:HEAD_DIM, :] * (1.0 / l0) - lam * (acc1[:HEAD_DIM, :] * (1.0 / l1))
        o = ot.T
        o = o * lax.rsqrt(jnp.mean(o * o, axis=-1, keepdims=True) + EPS) * nw_ref[...] * (1.0 - lambda_init)
        o_ref[...] = (o * _silu(g_ref[...].astype(F32))).astype(o_ref.dtype)

    init = ((jnp.full((1, tq), NEG_BIG, F32), jnp.zeros((vrows, tq), F32)),) * 2
    qs = query_operands(q_ref, cosq_ref, sinq_ref)

    @pl.when(qi == 0)
    def _():
        cmax = scores(qs, 0, chunk_len(0), 0)
        cm_ref[0] = cmax[0]
        cm_ref[1] = cmax[1]

    @pl.when(qi < seq // tq)
    def _():
        cmax = (cm_ref[0], cm_ref[1])
        state = init
        for c in range(n_chunks):
            if c < last:
                nxt = scores(qs, (c + 1) * tk, chunk_len(c + 1), (c + 1) % 2)
            else:
                nxt = scores(query_operands(qn_ref, cosn_ref, sinn_ref), 0, chunk_len(0), 0)
            vt = vtl_ref[...] if c == last else vtx_ref[c]
            state = consume(c % 2, chunk_len(c), vt, cmax, state)
            cmax = nxt
        cm_ref[0] = cmax[0]
        cm_ref[1] = cmax[1]
        finalize(state)

    @pl.when(qi >= seq // tq)
    def _():
        cmax = scores(qs, seq, ctx, 1)
        finalize(consume(1, ctx, vtl_ref[:, tk:tk + ctx], cmax, init))


def _attention(z, lams, cos, sin, nw, *, lambda_init, heads, seq, q_blk, k_blk, v_blk, g_blk):
    m = z.shape[0]
    ctx = m - seq
    tq = 256
    n_chunks = 4
    tk = seq // n_chunks
    assert seq % tq == 0 and ctx == tq, "one query block must hold exactly the context rows"
    assert seq % (n_chunks * LANES) == 0 and n_chunks % 2 == 0 and n_chunks >= 2
    nq = m // tq
    vrows = HEAD_DIM + 16
    lam_spec = pl.BlockSpec((1, HEAD_DIM // 2), lambda h, i: (0, 0))
    qrow = lambda base: pl.BlockSpec((tq, HEAD_DIM), lambda h, i: (i, base + h))
    next_row = lambda base, per_head: pl.BlockSpec(
        (tq, HEAD_DIM), lambda h, i: (jnp.minimum(i + 1, nq - 1), base + h * per_head))
    kcol = lambda base: pl.BlockSpec((m, HEAD_DIM), lambda h, i: (0, base + h))
    table = pl.BlockSpec((m, HEAD_DIM), lambda h, i: (0, 0), pipeline_mode=pl.Buffered(1))
    return pl.pallas_call(
        functools.partial(_attention_kernel, lambda_init=lambda_init, seq=seq, tq=tq, tk=tk),
        grid=(heads, nq),
        in_specs=[lam_spec, lam_spec, lam_spec, lam_spec,
                  qrow(q_blk),
                  pl.BlockSpec((tq, HEAD_DIM), lambda h, i: (i, 0)),
                  pl.BlockSpec((tq, HEAD_DIM), lambda h, i: (i, 0)),
                  next_row(q_blk, 1), next_row(0, 0), next_row(0, 0),
                  kcol(k_blk), kcol(v_blk), table, table,
                  qrow(g_blk),
                  pl.BlockSpec((1, HEAD_DIM), lambda h, i: (0, 0))],
        out_specs=pl.BlockSpec((tq, HEAD_DIM), lambda h, i: (i, h)),
        out_shape=jax.ShapeDtypeStruct((m, heads * HEAD_DIM), BF16),
        scratch_shapes=[pltpu.VMEM((m, HEAD_DIM), BF16),
                        pltpu.VMEM((n_chunks - 1, vrows, tk), BF16),
                        pltpu.VMEM((vrows, tk + ctx), BF16),
                        pltpu.VMEM((2, 2, tk + ctx, tq), F32),
                        pltpu.VMEM((2, 1, tq), F32)],
        compiler_params=_params(("arbitrary", "arbitrary")),
        name="diff_attention",
    )(*lams, z, cos, sin, z, cos, sin, z, z, cos, sin, z, nw)


def _mlp_kernel(u_ref, v_ref, g_ref, vnw_ref, ws_ref, bst_ref, o_ref, *, groups):
    tr, width = u_ref.shape
    gd = width // groups
    v = _gelu(v_ref[...].astype(F32))
    v = (v * lax.rsqrt(jnp.mean(v * v, axis=-1, keepdims=True) + EPS) * vnw_ref[...]).astype(BF16)
    for c in range(tr // CHUNK):
        rows = slice(c * CHUNK, (c + 1) * CHUNK)
        for g in range(groups):
            cols = slice(g * gd, (g + 1) * gd)
            mixed = jnp.dot(ws_ref[g].astype(BF16), v[rows, cols], preferred_element_type=F32) + bst_ref[:, g:g + 1]
            u = _gelu(u_ref[rows, cols].astype(F32))
            o_ref[rows, cols] = (u * mixed * _silu(g_ref[rows, cols].astype(F32))).astype(o_ref.dtype)


def _chunk_mlp(z, vnw, ws, bst, *, u_blk, width):
    m = z.shape[0]
    groups = ws.shape[0]
    tr = 256
    col = lambda b: pl.BlockSpec((tr, width), lambda i: (i, b))
    return pl.pallas_call(
        functools.partial(_mlp_kernel, groups=groups),
        grid=(m // tr,),
        in_specs=[col(u_blk), col(u_blk + 1), col(u_blk + 2),
                  pl.BlockSpec((1, width), lambda i: (0, 0)),
                  pl.BlockSpec((groups, CHUNK, CHUNK), lambda i: (0, 0, 0)),
                  pl.BlockSpec((CHUNK, groups), lambda i: (0, 0))],
        out_specs=pl.BlockSpec((tr, width), lambda i: (i, 0)),
        out_shape=jax.ShapeDtypeStruct((m, width), BF16),
        compiler_params=_params(("arbitrary",)),
        name="chunk_mlp",
    )(z, z, z, vnw, ws, bst)


def _out_proj_kernel(ret_ref, dif_ref, mlp_ref, wr_ref, wd_ref, wm_ref, x_ref, gate_ref, o_ref, *, seq, tm):
    y = jnp.dot(ret_ref[...], wr_ref[...].astype(BF16), preferred_element_type=F32)
    y = y + jnp.dot(dif_ref[...], wd_ref[...].astype(BF16), preferred_element_type=F32)
    y = y + jnp.dot(mlp_ref[...], wm_ref[...].astype(BF16), preferred_element_type=F32)
    row = pl.program_id(0) * tm + lax.broadcasted_iota(jnp.int32, y.shape, 0)
    gate = jnp.where(row >= seq, gate_ref[1:2, :], gate_ref[0:1, :])
    o_ref[...] = x_ref[...] + gate * y


def _out_proj(ret, dif, mlp, w_out, layer, xx, mod, *, seq):
    m, d = xx.shape
    rw, dw, mw = ret.shape[1], dif.shape[1], mlp.shape[1]
    assert rw == dw and (rw + dw) % mw == 0
    tm = _pick(m, (1056, 768, 640, 256))
    tn = _pick(d, (512, 256))
    gate_blk = 2 * d // tn
    return pl.pallas_call(
        functools.partial(_out_proj_kernel, seq=seq, tm=tm),
        grid=(m // tm, d // tn),
        in_specs=[
            pl.BlockSpec((tm, rw), lambda i, j: (i, 0)),
            pl.BlockSpec((tm, dw), lambda i, j: (i, 0)),
            pl.BlockSpec((tm, mw), lambda i, j: (i, 0)),
            pl.BlockSpec((None, rw, tn), lambda i, j: (layer, 0, j)),
            pl.BlockSpec((None, dw, tn), lambda i, j: (layer, 1, j)),
            pl.BlockSpec((None, mw, tn), lambda i, j: (layer, (rw + dw) // mw, j)),
            pl.BlockSpec((tm, tn), lambda i, j: (i, j)),
            pl.BlockSpec((8, tn), lambda i, j: (0, gate_blk + j)),
        ],
        out_specs=pl.BlockSpec((tm, tn), lambda i, j: (i, j)),
        out_shape=jax.ShapeDtypeStruct((m, d), F32),
        compiler_params=_params(("arbitrary", "arbitrary")),
        name="out_proj",
    )(ret, dif, mlp, w_out, w_out, w_out, xx, mod)


def _final_norm_kernel(x_ref, w_ref, o_ref):
    x = x_ref[...]
    o_ref[...] = x * lax.rsqrt(jnp.mean(x * x, axis=-1, keepdims=True) + EPS) * w_ref[...]


def _final_norm(xx, w, seq):
    d = xx.shape[1]
    tr = 256
    return pl.pallas_call(
        _final_norm_kernel,
        grid=(seq // tr,),
        in_specs=[pl.BlockSpec((tr, d), lambda i: (i, 0)), pl.BlockSpec((1, d), lambda i: (0, 0))],
        out_specs=pl.BlockSpec((tr, d), lambda i: (i, 0)),
        out_shape=jax.ShapeDtypeStruct((seq, d), F32),
        compiler_params=_params(("arbitrary",)),
        name="final_norm",
    )(xx, w)


def _rope_tables(seq, ctx, dim, repeats):
    rows = seq // GRID_W
    n_freq = dim // 4
    inv = ROPE_BASE ** (-jnp.arange(n_freq, dtype=F32) / n_freq)
    ang_r = jnp.arange(rows, dtype=F32)[:, None] * inv
    ang_c = jnp.arange(GRID_W, dtype=F32)[:, None] * inv

    def per_token(f):
        by_row = jnp.broadcast_to(f(ang_r)[:, None, :], (rows, GRID_W, n_freq))
        by_col = jnp.broadcast_to(f(ang_c)[None, :, :], (rows, GRID_W, n_freq))
        return jnp.concatenate([by_row, by_col], axis=-1).reshape(seq, 2 * n_freq)

    cos_t, sin_t = per_token(jnp.cos), per_token(jnp.sin)
    cos = jnp.concatenate([cos_t, cos_t], axis=-1)
    sin = jnp.concatenate([-sin_t, sin_t], axis=-1)
    cos = jnp.concatenate([jnp.tile(cos, (1, repeats)), jnp.ones((ctx, repeats * dim), F32)], axis=0)
    sin = jnp.concatenate([jnp.tile(sin, (1, repeats)), jnp.zeros((ctx, repeats * dim), F32)], axis=0)
    return cos, sin


def kernel(x, c, ctx, c_ctx, w_in, w_out, w_ada, b_ada, norm_w, ret_decay_fwd, ret_decay_bwd,
           lam_q1, lam_k1, lam_q2, lam_k2, dif_norm_w, mlp_vnorm_w, mlp_ws, mlp_bs, final_norm_w):
    batch, seq, d = x.shape
    assert batch == 1, "the kernels process one sequence"
    n_ctx = ctx.shape[1]
    depth = w_in.shape[0]
    heads = ret_decay_fwd.shape[1]
    ret_w = heads * HEAD_DIM
    dif_w = ret_w
    mlp_w = mlp_vnorm_w.shape[1]
    assert 2 * ret_w + mlp_w == d and w_in.shape[2] == 8 * ret_w + 3 * mlp_w

    rk_blk, rv_blk, dk_blk, dv_blk = 0, heads, 2 * heads, 3 * heads
    rq_blk, dq_blk, rg_blk, dg_blk = 4 * heads, 5 * heads, 6 * heads, 7 * heads
    mu_blk = 8 * ret_w // mlp_w

    xx = jnp.concatenate([x[0], ctx[0]], axis=0)
    cc = jnp.zeros((8, d), F32).at[0].set(c[0]).at[1].set(c_ctx)
    mods = _modulation(cc, w_ada, b_ada)
    cos_r, sin_r = _rope_tables(seq, n_ctx, HEAD_DIM, 1)
    cos_d, sin_d = _rope_tables(seq, n_ctx, HEAD_DIM // 2, 2)

    for l in range(depth):
        lambda_init = 0.8 - 0.6 * math.exp(-0.3 * l)
        h = _norm_mod(xx, norm_w[l][None, :], mods[l], seq)
        z = _in_proj(h, w_in, l, BF16)
        ret = _retention(z, ret_decay_fwd, ret_decay_bwd, cos_r, sin_r, layer=l, heads=heads, seq=seq,
                         q_blk=rq_blk, k_blk=rk_blk, v_blk=rv_blk, g_blk=rg_blk)
        lams = [a[l][None, :] for a in (lam_q1, lam_k1, lam_q2, lam_k2)]
        dif = _attention(z, lams, cos_d, sin_d, dif_norm_w[l][None, :], lambda_init=lambda_init,
                         heads=heads, seq=seq, q_blk=dq_blk, k_blk=dk_blk, v_blk=dv_blk, g_blk=dg_blk)
        mlp = _chunk_mlp(z, mlp_vnorm_w[l][None, :], mlp_ws[l], mlp_bs[l].T, u_blk=mu_blk, width=mlp_w)
        xx = _out_proj(ret, dif, mlp, w_out, l, xx, mods[l], seq=seq)

    return _final_norm(xx, final_norm_w[None, :], seq)[None]
```

```python
import functools
import math

import jax
import jax.numpy as jnp
from jax import lax
from jax.experimental import pallas as pl
from jax.experimental.pallas import tpu as pltpu

HEAD_DIM = 128
CHUNK = 128
GRID_W = 64
ROPE_BASE = 10000.0
EPS = 1e-6
LOG2E = 1.4426950408889634
LANES = 128
NEG_BIG = -1e30
RET_UNROLL = 8
VMEM_LIMIT = 56 * 1024 * 1024

BF16 = jnp.bfloat16
F32 = jnp.float32

NT_DIMS = (((1,), (1,)), ((), ()))
TN_DIMS = (((0,), (0,)), ((), ()))


def _params(semantics, **kwargs):
    return pltpu.CompilerParams(dimension_semantics=semantics, vmem_limit_bytes=VMEM_LIMIT, **kwargs)


def _pick(n, candidates):
    for c in candidates:
        if n % c == 0:
            return c
    return n


def _silu(x):
    return x * (1.0 / (1.0 + jnp.exp(-x)))


def _gelu(x):
    return 0.5 * x * (1.0 + lax.erf(x * (2.0 ** -0.5)))


def _mod_kernel(cc_ref, w_ref, b_ref, o_ref):
    a = _silu(cc_ref[...]).astype(BF16)
    o_ref[...] = jnp.dot(a, w_ref[...].astype(BF16), preferred_element_type=F32) + b_ref[...]


def _modulation(cc, w_ada, b_ada):
    depth, d, n3 = w_ada.shape
    tn = _pick(n3, (512, 256, 128))
    return pl.pallas_call(
        _mod_kernel,
        grid=(depth, n3 // tn),
        in_specs=[
            pl.BlockSpec((8, d), lambda l, j: (0, 0)),
            pl.BlockSpec((None, d, tn), lambda l, j: (l, 0, j)),
            pl.BlockSpec((None, 1, tn), lambda l, j: (l, 0, j)),
        ],
        out_specs=pl.BlockSpec((None, 8, tn), lambda l, j: (l, 0, j)),
        out_shape=jax.ShapeDtypeStruct((depth, 8, n3), F32),
        compiler_params=_params(("arbitrary", "arbitrary")),
        name="modulation",
    )(cc, w_ada, b_ada.reshape(depth, 1, n3))


def _norm_mod_kernel(x_ref, nw_ref, mod_ref, h_ref, *, d, n_latent_blocks):
    x = x_ref[...]
    y = x * lax.rsqrt(jnp.mean(x * x, axis=-1, keepdims=True) + EPS) * nw_ref[...]
    is_ctx = pl.program_id(0) >= n_latent_blocks
    shift = jnp.where(is_ctx, mod_ref[1:2, 0:d], mod_ref[0:1, 0:d])
    scale = jnp.where(is_ctx, mod_ref[1:2, d:2 * d], mod_ref[0:1, d:2 * d])
    h_ref[...] = (y * (1.0 + scale) + shift).astype(h_ref.dtype)


def _norm_mod(xx, nw, mod, seq):
    m, d = xx.shape
    tr = 256
    return pl.pallas_call(
        functools.partial(_norm_mod_kernel, d=d, n_latent_blocks=seq // tr),
        grid=(m // tr,),
        in_specs=[
            pl.BlockSpec((tr, d), lambda i: (i, 0)),
            pl.BlockSpec((1, d), lambda i: (0, 0)),
            pl.BlockSpec((8, 3 * d), lambda i: (0, 0)),
        ],
        out_specs=pl.BlockSpec((tr, d), lambda i: (i, 0)),
        out_shape=jax.ShapeDtypeStruct((m, d), BF16),
        compiler_params=_params(("arbitrary",)),
        name="norm_mod",
    )(xx, nw, mod)


def _matmul_kernel(a_ref, b_ref, o_ref):
    o_ref[...] = jnp.dot(a_ref[...], b_ref[...].astype(BF16), preferred_element_type=F32).astype(o_ref.dtype)


def _in_proj(h, w, layer, out_dtype):
    m, k = h.shape
    n = w.shape[2]
    tm = _pick(m, (1408, 1056, 768, 640, 256))
    tn = _pick(n, (512, 256))
    return pl.pallas_call(
        _matmul_kernel,
        grid=(m // tm, n // tn),
        in_specs=[
            pl.BlockSpec((tm, k), lambda i, j: (i, 0)),
            pl.BlockSpec((None, k, tn), lambda i, j: (layer, 0, j)),
        ],
        out_specs=pl.BlockSpec((tm, tn), lambda i, j: (i, j)),
        out_shape=jax.ShapeDtypeStruct((m, n), out_dtype),
        compiler_params=_params(("arbitrary", "arbitrary")),
        name="in_proj",
    )(h, w)


def _retention_kernel(decf_ref, decb_ref, q_ref, k_ref, v_ref, g_ref, cos_ref, sin_ref, o_ref,
                      ob_ref, *, layer, n_latent_chunks, n_ctx_chunks):
    head = pl.program_id(0)
    lgf = jnp.log1p(-jnp.exp(jnp.full((1, LANES), decf_ref[layer, head], F32)))
    lgb = jnp.log1p(-jnp.exp(jnp.full((1, LANES), decb_ref[layer, head], F32)))
    ri = lax.broadcasted_iota(jnp.int32, (CHUNK, CHUNK), 0)
    ci = lax.broadcasted_iota(jnp.int32, (CHUNK, CHUNK), 1)
    rel = (ri - ci).astype(F32)
    pos = ri.astype(F32)
    dmat = jnp.where(rel >= 0, jnp.exp(lgf * jnp.maximum(rel, 0.0)), jnp.exp(lgb * jnp.maximum(-rel, 0.0)))
    wfk = jnp.exp(lgf * (CHUNK - 1.0 - pos))
    wbk = jnp.exp(lgb * pos)
    dqf = jnp.exp(lgf * (pos + 1.0))
    dqb = jnp.exp(lgb * (CHUNK - pos))
    cdf = jnp.exp(lgf * float(CHUNK))
    cdb = jnp.exp(lgb * float(CHUNK))
    ret_scale = HEAD_DIM ** -0.5

    def rows_of(c):
        return pl.ds(pl.multiple_of(c * CHUNK, CHUNK), CHUNK)

    def roped(rows):
        cos = cos_ref[rows, :]
        sin = sin_ref[rows, :]
        q = q_ref[rows, :].astype(F32)
        k = k_ref[rows, :].astype(F32)
        qr = q * cos + pltpu.roll(q, HEAD_DIM // 2, 1) * sin
        kr = (k * cos + pltpu.roll(k, HEAD_DIM // 2, 1) * sin) * ret_scale
        return qr, kr

    def run_sequence(first, n, sf, sb):
        def bwd(t, sb):
            rows = rows_of(first + n - 1 - t)
            qr, kr = roped(rows)
            vb = v_ref[rows, :].astype(BF16)
            ob_ref[rows, :] = jnp.dot((qr * dqb).astype(BF16), sb.astype(BF16), preferred_element_type=F32)
            kv = lax.dot_general((kr * wbk).astype(BF16), vb, TN_DIMS, preferred_element_type=F32)
            return sb * cdb + kv

        sb = lax.fori_loop(0, n, bwd, sb, unroll=min(n, RET_UNROLL))

        def fwd(t, sf):
            rows = rows_of(first + t)
            qr, kr = roped(rows)
            vb = v_ref[rows, :].astype(BF16)
            a = lax.dot_general(qr.astype(BF16), kr.astype(BF16), NT_DIMS, preferred_element_type=F32) * dmat
            o = jnp.dot(a.astype(BF16), vb, preferred_element_type=F32)
            o = o + jnp.dot((qr * dqf).astype(BF16), sf.astype(BF16), preferred_element_type=F32)
            o = o + ob_ref[rows, :]
            o = o * lax.rsqrt(jnp.mean(o * o, axis=-1, keepdims=True) + EPS)
            o_ref[rows, :] = (o * _silu(g_ref[rows, :].astype(F32))).astype(o_ref.dtype)
            kv = lax.dot_general((kr * wfk).astype(BF16), vb, TN_DIMS, preferred_element_type=F32)
            return sf * cdf + kv

        sf = lax.fori_loop(0, n, fwd, sf, unroll=min(n, RET_UNROLL))
        return sf, sb

    zero = jnp.zeros((HEAD_DIM, HEAD_DIM), F32)
    sf, sb = run_sequence(n_latent_chunks, n_ctx_chunks, zero, zero)
    run_sequence(0, n_latent_chunks, sf, sb)


def _retention(z, dec_f, dec_b, cos, sin, *, layer, heads, seq, q_blk, k_blk, v_blk, g_blk):
    m = z.shape[0]
    col = lambda base: pl.BlockSpec((m, HEAD_DIM), lambda h: (0, base + h))
    smem = pl.BlockSpec(memory_space=pltpu.SMEM)
    full = pl.BlockSpec((m, HEAD_DIM), lambda h: (0, 0), pipeline_mode=pl.Buffered(1))
    return pl.pallas_call(
        functools.partial(_retention_kernel, layer=layer, n_latent_chunks=seq // CHUNK,
                          n_ctx_chunks=(m - seq) // CHUNK),
        grid=(heads,),
        in_specs=[smem, smem, col(q_blk), col(k_blk), col(v_blk), col(g_blk), full, full],
        out_specs=pl.BlockSpec((m, HEAD_DIM), lambda h: (0, h)),
        out_shape=jax.ShapeDtypeStruct((m, heads * HEAD_DIM), BF16),
        scratch_shapes=[pltpu.VMEM((m, HEAD_DIM), F32)],
        compiler_params=_params(("arbitrary",)),
        name="retention",
    )(dec_f, dec_b, z, z, z, z, cos, sin)


def _swap_half_pairs(x):
    lane = lax.broadcasted_iota(jnp.int32, x.shape, 1)
    return jnp.where(lane % 64 < 32, pltpu.roll(x, 96, 1), pltpu.roll(x, 32, 1))


def _dif_rope(x, cos, sin):
    return x * cos + _swap_half_pairs(x) * sin


def _attention_kernel(lq1_ref, lk1_ref, lq2_ref, lk2_ref, q_ref, cosq_ref, sinq_ref, qn_ref, cosn_ref, sinn_ref,
                      k_ref, v_ref, cosk_ref, sink_ref, gp_ref, nw_ref, o_ref,
                      ks_ref, vtx_ref, vtl_ref, s_ref, cm_ref, acc_ref, *, lambda_init, seq, tq, tk):
    qi = pl.program_id(1)
    m = k_ref.shape[0]
    ctx = m - seq
    vrows = vtl_ref.shape[0]
    n_chunks = seq // tk
    last = n_chunks - 1
    ones_rows = (lax.broadcasted_iota(jnp.int32, (vrows - HEAD_DIM, 1), 0) == 0).astype(BF16)

    def chunk_len(c):
        return tk + ctx if c == last else tk

    @pl.when(qi == 0)
    def _():
        def prep_rows(rows, n):
            k = _dif_rope(k_ref[rows, :].astype(F32), cosk_ref[rows, :], sink_ref[rows, :])
            ks_ref[rows, :] = k.astype(BF16)
            vt = v_ref[rows, :].astype(F32).T.astype(BF16)
            return jnp.concatenate([vt, jnp.broadcast_to(ones_rows, (vrows - HEAD_DIM, n))], axis=0)

        def prep(c, carry):
            vtx_ref[c] = prep_rows(pl.ds(pl.multiple_of(c * tk, tk), tk), tk)
            return carry
        lax.fori_loop(0, last, prep, 0)
        vtl_ref[:, 0:tk] = prep_rows(pl.ds(last * tk, tk), tk)
        vtl_ref[:, tk:tk + ctx] = prep_rows(pl.ds(seq, ctx), ctx)

    def query_operands(ref, cos_ref, sin_ref):
        q = _dif_rope(ref[...].astype(F32), cos_ref[...], sin_ref[...]) * ((HEAD_DIM // 2) ** -0.5 * LOG2E)
        lane = lax.broadcasted_iota(jnp.int32, q.shape, 1)
        q0 = jnp.where(lane < HEAD_DIM // 2, q, 0.0).astype(BF16)
        q1 = jnp.where(lane >= HEAD_DIM // 2, q, 0.0).astype(BF16)
        return q0, q1

    def scores(qs, row0, n, slot):
        kb = ks_ref[pl.ds(row0, n), :]
        cmax = []
        for comp, qm in enumerate(qs):
            s = lax.dot_general(kb, qm, NT_DIMS, preferred_element_type=F32)
            s_ref[slot, comp, 0:n, :] = s
            cmax.append(jnp.max(s, axis=0, keepdims=True))
        return tuple(cmax)

    def consume(slot, n, vt, cmax, state):
        new = []
        for comp in range(2):
            mx, acc = state[comp]
            mn = jnp.maximum(mx, cmax[comp])
            p = jnp.exp2(s_ref[slot, comp, 0:n, :] - mn).astype(BF16)
            acc = jnp.exp2(mx - mn) * acc + jnp.dot(vt, p, preferred_element_type=F32)
            new.append((mn, acc))
        return tuple(new)

    def finalize_previous():
        acc0 = acc_ref[0]
        acc1 = acc_ref[1]
        lam = (jnp.exp(jnp.sum(lq1_ref[...] * lk1_ref[...], axis=-1, keepdims=True))
               - jnp.exp(jnp.sum(lq2_ref[...] * lk2_ref[...], axis=-1, keepdims=True)) + lambda_init)
        l0 = acc0[HEAD_DIM:HEAD_DIM + 1, :]
        l1 = acc1[HEAD_DIM:HEAD_DIM + 1, :]
        ot = acc0[:HEAD_DIM, :] * (1.0 / l0) - lam * (acc1[:HEAD_DIM, :] * (1.0 / l1))
        o = ot.T
        o = o * lax.rsqrt(jnp.mean(o * o, axis=-1, keepdims=True) + EPS) * nw_ref[...] * (1.0 - lambda_init)
        o_ref[...] = (o * _silu(gp_ref[...].astype(F32))).astype(o_ref.dtype)

    def leave_sums(state):
        (_, acc0), (_, acc1) = state
        acc_ref[0] = acc0
        acc_ref[1] = acc1

    init = ((jnp.full((1, tq), NEG_BIG, F32), jnp.zeros((vrows, tq), F32)),) * 2
    qs = query_operands(q_ref, cosq_ref, sinq_ref)
    n_latent_blocks = seq // tq

    @pl.when(qi == 0)
    def _():
        cmax = scores(qs, 0, chunk_len(0), 0)
        cm_ref[0] = cmax[0]
        cm_ref[1] = cmax[1]
        acc_ref[...] = jnp.ones(acc_ref.shape, F32)

    @pl.when(qi < n_latent_blocks)
    def _():
        finalize_previous()
        cmax = (cm_ref[0], cm_ref[1])
        state = init
        for c in range(n_chunks):
            if c < last:
                nxt = scores(qs, (c + 1) * tk, chunk_len(c + 1), (c + 1) % 2)
            else:
                nxt = scores(query_operands(qn_ref, cosn_ref, sinn_ref), 0, chunk_len(0), 0)
            vt = vtl_ref[...] if c == last else vtx_ref[c]
            state = consume(c % 2, chunk_len(c), vt, cmax, state)
            cmax = nxt
        cm_ref[0] = cmax[0]
        cm_ref[1] = cmax[1]
        leave_sums(state)

    @pl.when(qi == n_latent_blocks)
    def _():
        finalize_previous()
        cmax = scores(qs, seq, ctx, 1)
        leave_sums(consume(1, ctx, vtl_ref[:, tk:tk + ctx], cmax, init))

    @pl.when(qi > n_latent_blocks)
    def _():
        finalize_previous()


def _attention(z, lams, cos, sin, nw, *, lambda_init, heads, seq, q_blk, k_blk, v_blk, g_blk):
    m = z.shape[0]
    ctx = m - seq
    tq = 256
    n_chunks = 4
    tk = seq // n_chunks
    assert seq % tq == 0 and ctx == tq, "one query block must hold exactly the context rows"
    assert seq % (n_chunks * LANES) == 0 and n_chunks % 2 == 0 and n_chunks >= 2
    nq = m // tq
    vrows = HEAD_DIM + 16
    lam_spec = pl.BlockSpec((1, HEAD_DIM // 2), lambda h, i: (0, 0))
    this_row = lambda base, per_head: pl.BlockSpec(
        (tq, HEAD_DIM), lambda h, i: (jnp.minimum(i, nq - 1), base + h * per_head))
    next_row = lambda base, per_head: pl.BlockSpec(
        (tq, HEAD_DIM), lambda h, i: (jnp.minimum(i + 1, nq - 1), base + h * per_head))
    prev_row = lambda base: pl.BlockSpec((tq, HEAD_DIM), lambda h, i: (jnp.maximum(i - 1, 0), base + h))
    kcol = lambda base: pl.BlockSpec((m, HEAD_DIM), lambda h, i: (0, base + h))
    table = pl.BlockSpec((m, HEAD_DIM), lambda h, i: (0, 0), pipeline_mode=pl.Buffered(1))
    return pl.pallas_call(
        functools.partial(_attention_kernel, lambda_init=lambda_init, seq=seq, tq=tq, tk=tk),
        grid=(heads, nq + 1),
        in_specs=[lam_spec, lam_spec, lam_spec, lam_spec,
                  this_row(q_blk, 1), this_row(0, 0), this_row(0, 0),
                  next_row(q_blk, 1), next_row(0, 0), next_row(0, 0),
                  kcol(k_blk), kcol(v_blk), table, table,
                  prev_row(g_blk),
                  pl.BlockSpec((1, HEAD_DIM), lambda h, i: (0, 0))],
        out_specs=prev_row(0),
        out_shape=jax.ShapeDtypeStruct((m, heads * HEAD_DIM), BF16),
        scratch_shapes=[pltpu.VMEM((m, HEAD_DIM), BF16),
                        pltpu.VMEM((n_chunks - 1, vrows, tk), BF16),
                        pltpu.VMEM((vrows, tk + ctx), BF16),
                        pltpu.VMEM((2, 2, tk + ctx, tq), F32),
                        pltpu.VMEM((2, 1, tq), F32),
                        pltpu.VMEM((2, vrows, tq), F32)],
        compiler_params=_params(("arbitrary", "arbitrary")),
        name="diff_attention",
    )(*lams, z, cos, sin, z, cos, sin, z, z, cos, sin, z, nw)


def _mlp_kernel(u_ref, v_ref, g_ref, vnw_ref, ws_ref, bst_ref, o_ref, *, groups):
    tr, width = u_ref.shape
    gd = width // groups
    v = _gelu(v_ref[...].astype(F32))
    v = (v * lax.rsqrt(jnp.mean(v * v, axis=-1, keepdims=True) + EPS) * vnw_ref[...]).astype(BF16)
    for c in range(tr // CHUNK):
        rows = slice(c * CHUNK, (c + 1) * CHUNK)
        for g in range(groups):
            cols = slice(g * gd, (g + 1) * gd)
            mixed = jnp.dot(ws_ref[g].astype(BF16), v[rows, cols], preferred_element_type=F32) + bst_ref[:, g:g + 1]
            u = _gelu(u_ref[rows, cols].astype(F32))
            o_ref[rows, cols] = (u * mixed * _silu(g_ref[rows, cols].astype(F32))).astype(o_ref.dtype)


def _chunk_mlp(z, vnw, ws, bst, *, u_blk, width):
    m = z.shape[0]
    groups = ws.shape[0]
    tr = 256
    col = lambda b: pl.BlockSpec((tr, width), lambda i: (i, b))
    return pl.pallas_call(
        functools.partial(_mlp_kernel, groups=groups),
        grid=(m // tr,),
        in_specs=[col(u_blk), col(u_blk + 1), col(u_blk + 2),
                  pl.BlockSpec((1, width), lambda i: (0, 0)),
                  pl.BlockSpec((groups, CHUNK, CHUNK), lambda i: (0, 0, 0)),
                  pl.BlockSpec((CHUNK, groups), lambda i: (0, 0))],
        out_specs=pl.BlockSpec((tr, width), lambda i: (i, 0)),
        out_shape=jax.ShapeDtypeStruct((m, width), BF16),
        compiler_params=_params(("arbitrary",)),
        name="chunk_mlp",
    )(z, z, z, vnw, ws, bst)


def _out_proj_kernel(ret_ref, dif_ref, mlp_ref, wr_ref, wd_ref, wm_ref, x_ref, gate_ref, o_ref, *, seq, tm):
    y = jnp.dot(ret_ref[...], wr_ref[...].astype(BF16), preferred_element_type=F32)
    y = y + jnp.dot(dif_ref[...], wd_ref[...].astype(BF16), preferred_element_type=F32)
    y = y + jnp.dot(mlp_ref[...], wm_ref[...].astype(BF16), preferred_element_type=F32)
    row = pl.program_id(0) * tm + lax.broadcasted_iota(jnp.int32, y.shape, 0)
    gate = jnp.where(row >= seq, gate_ref[1:2, :], gate_ref[0:1, :])
    o_ref[...] = x_ref[...] + gate * y


def _out_proj(ret, dif, mlp, w_out, layer, xx, mod, *, seq):
    m, d = xx.shape
    rw, dw, mw = ret.shape[1], dif.shape[1], mlp.shape[1]
    assert rw == dw and (rw + dw) % mw == 0
    tm = _pick(m, (1056, 768, 640, 256))
    tn = _pick(d, (512, 256))
    gate_blk = 2 * d // tn
    return pl.pallas_call(
        functools.partial(_out_proj_kernel, seq=seq, tm=tm),
        grid=(m // tm, d // tn),
        in_specs=[
            pl.BlockSpec((tm, rw), lambda i, j: (i, 0)),
            pl.BlockSpec((tm, dw), lambda i, j: (i, 0)),
            pl.BlockSpec((tm, mw), lambda i, j: (i, 0)),
            pl.BlockSpec((None, rw, tn), lambda i, j: (layer, 0, j)),
            pl.BlockSpec((None, dw, tn), lambda i, j: (layer, 1, j)),
            pl.BlockSpec((None, mw, tn), lambda i, j: (layer, (rw + dw) // mw, j)),
            pl.BlockSpec((tm, tn), lambda i, j: (i, j)),
            pl.BlockSpec((8, tn), lambda i, j: (0, gate_blk + j)),
        ],
        out_specs=pl.BlockSpec((tm, tn), lambda i, j: (i, j)),
        out_shape=jax.ShapeDtypeStruct((m, d), F32),
        compiler_params=_params(("arbitrary", "arbitrary")),
        name="out_proj",
    )(ret, dif, mlp, w_out, w_out, w_out, xx, mod)


def _final_norm_kernel(x_ref, w_ref, o_ref):
    x = x_ref[...]
    o_ref[...] = x * lax.rsqrt(jnp.mean(x * x, axis=-1, keepdims=True) + EPS) * w_ref[...]


def _final_norm(xx, w, seq):
    d = xx.shape[1]
    tr = 256
    return pl.pallas_call(
        _final_norm_kernel,
        grid=(seq // tr,),
        in_specs=[pl.BlockSpec((tr, d), lambda i: (i, 0)), pl.BlockSpec((1, d), lambda i: (0, 0))],
        out_specs=pl.BlockSpec((tr, d), lambda i: (i, 0)),
        out_shape=jax.ShapeDtypeStruct((seq, d), F32),
        compiler_params=_params(("arbitrary",)),
        name="final_norm",
    )(xx, w)


def _rope_tables(seq, ctx, dim, repeats):
    rows = seq // GRID_W
    n_freq = dim // 4
    inv = ROPE_BASE ** (-jnp.arange(n_freq, dtype=F32) / n_freq)
    ang_r = jnp.arange(rows, dtype=F32)[:, None] * inv
    ang_c = jnp.arange(GRID_W, dtype=F32)[:, None] * inv

    def per_token(f):
        by_row = jnp.broadcast_to(f(ang_r)[:, None, :], (rows, GRID_W, n_freq))
        by_col = jnp.broadcast_to(f(ang_c)[None, :, :], (rows, GRID_W, n_freq))
        return jnp.concatenate([by_row, by_col], axis=-1).reshape(seq, 2 * n_freq)

    cos_t, sin_t = per_token(jnp.cos), per_token(jnp.sin)
    cos = jnp.concatenate([cos_t, cos_t], axis=-1)
    sin = jnp.concatenate([-sin_t, sin_t], axis=-1)
    cos = jnp.concatenate([jnp.tile(cos, (1, repeats)), jnp.ones((ctx, repeats * dim), F32)], axis=0)
    sin = jnp.concatenate([jnp.tile(sin, (1, repeats)), jnp.zeros((ctx, repeats * dim), F32)], axis=0)
    return cos, sin


def kernel(x, c, ctx, c_ctx, w_in, w_out, w_ada, b_ada, norm_w, ret_decay_fwd, ret_decay_bwd,
           lam_q1, lam_k1, lam_q2, lam_k2, dif_norm_w, mlp_vnorm_w, mlp_ws, mlp_bs, final_norm_w):
    batch, seq, d = x.shape
    assert batch == 1, "the kernels process one sequence"
    n_ctx = ctx.shape[1]
    depth = w_in.shape[0]
    heads = ret_decay_fwd.shape[1]
    ret_w = heads * HEAD_DIM
    dif_w = ret_w
    mlp_w = mlp_vnorm_w.shape[1]
    assert 2 * ret_w + mlp_w == d and w_in.shape[2] == 8 * ret_w + 3 * mlp_w

    rk_blk, rv_blk, dk_blk, dv_blk = 0, heads, 2 * heads, 3 * heads
    rq_blk, dq_blk, rg_blk, dg_blk = 4 * heads, 5 * heads, 6 * heads, 7 * heads
    mu_blk = 8 * ret_w // mlp_w

    xx = jnp.concatenate([x[0], ctx[0]], axis=0)
    cc = jnp.zeros((8, d), F32).at[0].set(c[0]).at[1].set(c_ctx)
    mods = _modulation(cc, w_ada, b_ada)
    cos_r, sin_r = _rope_tables(seq, n_ctx, HEAD_DIM, 1)
    cos_d, sin_d = _rope_tables(seq, n_ctx, HEAD_DIM // 2, 2)

    for l in range(depth):
        lambda_init = 0.8 - 0.6 * math.exp(-0.3 * l)
        h = _norm_mod(xx, norm_w[l][None, :], mods[l], seq)
        z = _in_proj(h, w_in, l, BF16)
        ret = _retention(z, ret_decay_fwd, ret_decay_bwd, cos_r, sin_r, layer=l, heads=heads, seq=seq,
                         q_blk=rq_blk, k_blk=rk_blk, v_blk=rv_blk, g_blk=rg_blk)
        lams = [a[l][None, :] for a in (lam_q1, lam_k1, lam_q2, lam_k2)]
        dif = _attention(z, lams, cos_d, sin_d, dif_norm_w[l][None, :], lambda_init=lambda_init,
                         heads=heads, seq=seq, q_blk=dq_blk, k_blk=dk_blk, v_blk=dv_blk, g_blk=dg_blk)
        mlp = _chunk_mlp(z, mlp_vnorm_w[l][None, :], mlp_ws[l], mlp_bs[l].T, u_blk=mu_blk, width=mlp_w)
        xx = _out_proj(ret, dif, mlp, w_out, l, xx, mods[l], seq=seq)

    return _final_norm(xx, final_norm_w[None, :], seq)[None]
```

```python
import functools
import math

import jax
import jax.numpy as jnp
from jax import lax
from jax.experimental import pallas as pl
from jax.experimental.pallas import tpu as pltpu

HEAD_DIM = 128
CHUNK = 128
GRID_W = 64
ROPE_BASE = 10000.0
EPS = 1e-6
LOG2E = 1.4426950408889634
LANES = 128
NEG_BIG = -1e30
RET_UNROLL = 8
VMEM_LIMIT = 56 * 1024 * 1024

BF16 = jnp.bfloat16
F32 = jnp.float32

NT_DIMS = (((1,), (1,)), ((), ()))
TN_DIMS = (((0,), (0,)), ((), ()))


def _params(semantics, **kwargs):
    return pltpu.CompilerParams(dimension_semantics=semantics, vmem_limit_bytes=VMEM_LIMIT, **kwargs)


def _pick(n, candidates):
    for c in candidates:
        if n % c == 0:
            return c
    return n


def _silu(x):
    return x * (1.0 / (1.0 + jnp.exp(-x)))


def _gelu(x):
    return 0.5 * x * (1.0 + lax.erf(x * (2.0 ** -0.5)))


def _mod_kernel(cc_ref, w_ref, b_ref, o_ref):
    a = _silu(cc_ref[...]).astype(BF16)
    o_ref[...] = jnp.dot(a, w_ref[...].astype(BF16), preferred_element_type=F32) + b_ref[...]


def _modulation(cc, w_ada, b_ada):
    depth, d, n3 = w_ada.shape
    tn = _pick(n3, (512, 256, 128))
    return pl.pallas_call(
        _mod_kernel,
        grid=(depth, n3 // tn),
        in_specs=[
            pl.BlockSpec((8, d), lambda l, j: (0, 0)),
            pl.BlockSpec((None, d, tn), lambda l, j: (l, 0, j)),
            pl.BlockSpec((None, 1, tn), lambda l, j: (l, 0, j)),
        ],
        out_specs=pl.BlockSpec((None, 8, tn), lambda l, j: (l, 0, j)),
        out_shape=jax.ShapeDtypeStruct((depth, 8, n3), F32),
        compiler_params=_params(("arbitrary", "arbitrary")),
        name="modulation",
    )(cc, w_ada, b_ada.reshape(depth, 1, n3))


def _norm_mod_kernel(*refs, d, n_latent_blocks, joins_streams):
    is_ctx = pl.program_id(0) >= n_latent_blocks
    if joins_streams:
        x_ref, c_ref, nw_ref, mod_ref, h_ref, xx_ref = refs
        x = jnp.where(is_ctx, c_ref[...], x_ref[...])
        xx_ref[...] = x
    else:
        x_ref, nw_ref, mod_ref, h_ref = refs
        x = x_ref[...]
    y = x * lax.rsqrt(jnp.mean(x * x, axis=-1, keepdims=True) + EPS) * nw_ref[...]
    shift = jnp.where(is_ctx, mod_ref[1:2, 0:d], mod_ref[0:1, 0:d])
    scale = jnp.where(is_ctx, mod_ref[1:2, d:2 * d], mod_ref[0:1, d:2 * d])
    h_ref[...] = (y * (1.0 + scale) + shift).astype(h_ref.dtype)


def _norm_mod(streams, nw, mod, seq):
    joins_streams = isinstance(streams, tuple)
    tr = 256
    if joins_streams:
        x, c = streams
        d = x.shape[1]
        m = seq + c.shape[0]
        n_latent_blocks = seq // tr
        operands = (x, c)
        stream_specs = [pl.BlockSpec((tr, d), lambda i: (jnp.minimum(i, n_latent_blocks - 1), 0)),
                        pl.BlockSpec((tr, d), lambda i: (jnp.maximum(i - n_latent_blocks, 0), 0))]
    else:
        m, d = streams.shape
        operands = (streams,)
        stream_specs = [pl.BlockSpec((tr, d), lambda i: (i, 0))]
    row_block = pl.BlockSpec((tr, d), lambda i: (i, 0))
    out = pl.pallas_call(
        functools.partial(_norm_mod_kernel, d=d, n_latent_blocks=seq // tr, joins_streams=joins_streams),
        grid=(m // tr,),
        in_specs=stream_specs + [pl.BlockSpec((1, d), lambda i: (0, 0)),
                                 pl.BlockSpec((8, 3 * d), lambda i: (0, 0))],
        out_specs=(row_block, row_block) if joins_streams else row_block,
        out_shape=((jax.ShapeDtypeStruct((m, d), BF16), jax.ShapeDtypeStruct((m, d), F32))
                   if joins_streams else jax.ShapeDtypeStruct((m, d), BF16)),
        compiler_params=_params(("arbitrary",)),
        name="norm_mod",
    )(*operands, nw, mod)
    return out


def _matmul_kernel(a_ref, b_ref, o_ref):
    o_ref[...] = jnp.dot(a_ref[...], b_ref[...].astype(BF16), preferred_element_type=F32).astype(o_ref.dtype)


def _in_proj(h, w, layer, out_dtype):
    m, k = h.shape
    n = w.shape[2]
    tm = _pick(m, (1408, 1056, 768, 640, 256))
    tn = _pick(n, (512, 256))
    return pl.pallas_call(
        _matmul_kernel,
        grid=(m // tm, n // tn),
        in_specs=[
            pl.BlockSpec((tm, k), lambda i, j: (i, 0)),
            pl.BlockSpec((None, k, tn), lambda i, j: (layer, 0, j)),
        ],
        out_specs=pl.BlockSpec((tm, tn), lambda i, j: (i, j)),
        out_shape=jax.ShapeDtypeStruct((m, n), out_dtype),
        compiler_params=_params(("arbitrary", "arbitrary")),
        name="in_proj",
    )(h, w)


def _retention_kernel(decf_ref, decb_ref, q_ref, k_ref, v_ref, g_ref, cos_ref, sin_ref, o_ref,
                      sf_ref, sb_ref, kvb_ref, lhs_ref, *, layer, n_latent_chunks, n_ctx_chunks):
    head = pl.program_id(0)
    lgf = jnp.log1p(-jnp.exp(jnp.full((1, LANES), decf_ref[layer, head], F32)))
    lgb = jnp.log1p(-jnp.exp(jnp.full((1, LANES), decb_ref[layer, head], F32)))
    ri = lax.broadcasted_iota(jnp.int32, (CHUNK, CHUNK), 0)
    ci = lax.broadcasted_iota(jnp.int32, (CHUNK, CHUNK), 1)
    rel = (ri - ci).astype(F32)
    pos = ri.astype(F32)
    dmat = jnp.where(rel >= 0, jnp.exp(lgf * jnp.maximum(rel, 0.0)), jnp.exp(lgb * jnp.maximum(-rel, 0.0)))
    wfk = jnp.exp(lgf * (CHUNK - 1.0 - pos))
    wbk = jnp.exp(lgb * pos)
    dqf = jnp.exp(lgf * (pos + 1.0))
    dqb = jnp.exp(lgb * (CHUNK - pos))
    cdf = jnp.exp(lgf * float(CHUNK))
    cdb = jnp.exp(lgb * float(CHUNK))
    ret_scale = HEAD_DIM ** -0.5

    def rows_of(c):
        return pl.ds(pl.multiple_of(c * CHUNK, CHUNK), CHUNK)

    def rope(ref, rows):
        x = ref[rows, :].astype(F32)
        return x * cos_ref[rows, :] + pltpu.roll(x, HEAD_DIM // 2, 1) * sin_ref[rows, :]

    def scan_states(first, n, sf, sb):
        def up(t, sf):
            c = first + t
            rows = rows_of(c)
            qr = rope(q_ref, rows)
            kr = rope(k_ref, rows) * ret_scale
            a = lax.dot_general(qr.astype(BF16), kr.astype(BF16), NT_DIMS, preferred_element_type=F32) * dmat
            lhs_ref[c] = jnp.concatenate(
                [a.astype(BF16), (qr * dqf).astype(BF16), (qr * dqb).astype(BF16)], axis=1)
            kk = jnp.concatenate([(kr * wfk).astype(BF16), (kr * wbk).astype(BF16)], axis=1)
            kv = lax.dot_general(kk, v_ref[rows, :].astype(BF16), TN_DIMS, preferred_element_type=F32)
            sf_ref[c] = sf.astype(BF16)
            kvb_ref[c] = kv[HEAD_DIM:, :]
            return sf * cdf + kv[:HEAD_DIM, :]

        sf = lax.fori_loop(0, n, up, sf, unroll=min(n, RET_UNROLL))

        def down(t, sb):
            c = first + n - 1 - t
            sb_ref[c] = sb.astype(BF16)
            return sb * cdb + kvb_ref[c]

        sb = lax.fori_loop(0, n, down, sb, unroll=min(n, RET_UNROLL))
        return sf, sb

    def outputs(first, n):
        def one(t, carry):
            c = first + t
            rows = rows_of(c)
            rhs = jnp.concatenate([v_ref[rows, :].astype(BF16), sf_ref[c], sb_ref[c]], axis=0)
            o = jnp.dot(lhs_ref[c], rhs, preferred_element_type=F32)
            o = o * lax.rsqrt(jnp.mean(o * o, axis=-1, keepdims=True) + EPS)
            o_ref[rows, :] = (o * _silu(g_ref[rows, :].astype(F32))).astype(o_ref.dtype)
            return carry

        lax.fori_loop(0, n, one, 0, unroll=min(n, RET_UNROLL))

    zero = jnp.zeros((HEAD_DIM, HEAD_DIM), F32)
    sf, sb = scan_states(n_latent_chunks, n_ctx_chunks, zero, zero)
    scan_states(0, n_latent_chunks, sf, sb)
    outputs(0, n_latent_chunks)
    outputs(n_latent_chunks, n_ctx_chunks)


def _retention(z, dec_f, dec_b, cos, sin, *, layer, heads, seq, q_blk, k_blk, v_blk, g_blk):
    m = z.shape[0]
    col = lambda base: pl.BlockSpec((m, HEAD_DIM), lambda h: (0, base + h))
    smem = pl.BlockSpec(memory_space=pltpu.SMEM)
    full = pl.BlockSpec((m, HEAD_DIM), lambda h: (0, 0), pipeline_mode=pl.Buffered(1))
    return pl.pallas_call(
        functools.partial(_retention_kernel, layer=layer, n_latent_chunks=seq // CHUNK,
                          n_ctx_chunks=(m - seq) // CHUNK),
        grid=(heads,),
        in_specs=[smem, smem, col(q_blk), col(k_blk), col(v_blk), col(g_blk), full, full],
        out_specs=pl.BlockSpec((m, HEAD_DIM), lambda h: (0, h)),
        out_shape=jax.ShapeDtypeStruct((m, heads * HEAD_DIM), BF16),
        scratch_shapes=[pltpu.VMEM((m // CHUNK, HEAD_DIM, HEAD_DIM), BF16),
                        pltpu.VMEM((m // CHUNK, HEAD_DIM, HEAD_DIM), BF16),
                        pltpu.VMEM((m // CHUNK, HEAD_DIM, HEAD_DIM), F32),
                        pltpu.VMEM((m // CHUNK, CHUNK, 3 * HEAD_DIM), BF16)],
        compiler_params=_params(("arbitrary",)),
        name="retention",
    )(dec_f, dec_b, z, z, z, z, cos, sin)


def _swap_half_pairs(x):
    lane = lax.broadcasted_iota(jnp.int32, x.shape, 1)
    return jnp.where(lane % 64 < 32, pltpu.roll(x, 96, 1), pltpu.roll(x, 32, 1))


def _dif_rope(x, cos, sin):
    return x * cos + _swap_half_pairs(x) * sin


def _attention_kernel(lq1_ref, lk1_ref, lq2_ref, lk2_ref, q_ref, cosq_ref, sinq_ref, qn_ref, cosn_ref, sinn_ref,
                      k_ref, v_ref, cosk_ref, sink_ref, gp_ref, nw_ref, o_ref,
                      ks_ref, vtx_ref, vtl_ref, s_ref, cm_ref, acc_ref, *, lambda_init, seq, tq, tk):
    qi = pl.program_id(1)
    m = k_ref.shape[0]
    ctx = m - seq
    vrows = vtl_ref.shape[0]
    n_chunks = seq // tk
    last = n_chunks - 1
    ones_rows = (lax.broadcasted_iota(jnp.int32, (vrows - HEAD_DIM, 1), 0) == 0).astype(BF16)

    def chunk_len(c):
        return tk + ctx if c == last else tk

    @pl.when(qi == 0)
    def _():
        def prep_rows(rows, n):
            k = _dif_rope(k_ref[rows, :].astype(F32), cosk_ref[rows, :], sink_ref[rows, :])
            ks_ref[rows, :] = k.astype(BF16)
            vt = v_ref[rows, :].astype(F32).T.astype(BF16)
            return jnp.concatenate([vt, jnp.broadcast_to(ones_rows, (vrows - HEAD_DIM, n))], axis=0)

        def prep(c, carry):
            vtx_ref[c] = prep_rows(pl.ds(pl.multiple_of(c * tk, tk), tk), tk)
            return carry
        lax.fori_loop(0, last, prep, 0)
        vtl_ref[:, 0:tk] = prep_rows(pl.ds(last * tk, tk), tk)
        vtl_ref[:, tk:tk + ctx] = prep_rows(pl.ds(seq, ctx), ctx)

    def query_operands(ref, cos_ref, sin_ref):
        q = _dif_rope(ref[...].astype(F32), cos_ref[...], sin_ref[...]) * ((HEAD_DIM // 2) ** -0.5 * LOG2E)
        lane = lax.broadcasted_iota(jnp.int32, q.shape, 1)
        q0 = jnp.where(lane < HEAD_DIM // 2, q, 0.0).astype(BF16)
        q1 = jnp.where(lane >= HEAD_DIM // 2, q, 0.0).astype(BF16)
        return q0, q1

    def scores(qs, row0, n, slot):
        kb = ks_ref[pl.ds(row0, n), :]
        cmax = []
        for comp, qm in enumerate(qs):
            s = lax.dot_general(kb, qm, NT_DIMS, preferred_element_type=F32)
            s_ref[slot, comp, 0:n, :] = s
            cmax.append(jnp.max(s, axis=0, keepdims=True))
        return tuple(cmax)

    def consume(slot, n, vt, cmax, state):
        new = []
        for comp in range(2):
            mx, acc = state[comp]
            mn = jnp.maximum(mx, cmax[comp])
            p = jnp.exp2(s_ref[slot, comp, 0:n, :] - mn).astype(BF16)
            acc = jnp.exp2(mx - mn) * acc + jnp.dot(vt, p, preferred_element_type=F32)
            new.append((mn, acc))
        return tuple(new)

    def finalize_previous():
        acc0 = acc_ref[0]
        acc1 = acc_ref[1]
        lam = (jnp.exp(jnp.sum(lq1_ref[...] * lk1_ref[...], axis=-1, keepdims=True))
               - jnp.exp(jnp.sum(lq2_ref[...] * lk2_ref[...], axis=-1, keepdims=True)) + lambda_init)
        l0 = acc0[HEAD_DIM:HEAD_DIM + 1, :]
        l1 = acc1[HEAD_DIM:HEAD_DIM + 1, :]
        ot = acc0[:HEAD_DIM, :] * (1.0 / l0) - lam * (acc1[:HEAD_DIM, :] * (1.0 / l1))
        o = ot.T
        o = o * lax.rsqrt(jnp.mean(o * o, axis=-1, keepdims=True) + EPS) * nw_ref[...] * (1.0 - lambda_init)
        o_ref[...] = (o * _silu(gp_ref[...].astype(F32))).astype(o_ref.dtype)

    def leave_sums(state):
        (_, acc0), (_, acc1) = state
        acc_ref[0] = acc0
        acc_ref[1] = acc1

    init = ((jnp.full((1, tq), NEG_BIG, F32), jnp.zeros((vrows, tq), F32)),) * 2
    qs = query_operands(q_ref, cosq_ref, sinq_ref)
    n_latent_blocks = seq // tq

    @pl.when(qi == 0)
    def _():
        cmax = scores(qs, 0, chunk_len(0), 0)
        cm_ref[0] = cmax[0]
        cm_ref[1] = cmax[1]
        acc_ref[...] = jnp.ones(acc_ref.shape, F32)

    @pl.when(qi < n_latent_blocks)
    def _():
        finalize_previous()
        cmax = (cm_ref[0], cm_ref[1])
        state = init
        for c in range(n_chunks):
            if c < last:
                nxt = scores(qs, (c + 1) * tk, chunk_len(c + 1), (c + 1) % 2)
            else:
                nxt = scores(query_operands(qn_ref, cosn_ref, sinn_ref), 0, chunk_len(0), 0)
            vt = vtl_ref[...] if c == last else vtx_ref[c]
            state = consume(c % 2, chunk_len(c), vt, cmax, state)
            cmax = nxt
        cm_ref[0] = cmax[0]
        cm_ref[1] = cmax[1]
        leave_sums(state)

    @pl.when(qi == n_latent_blocks)
    def _():
        finalize_previous()
        cmax = scores(qs, seq, ctx, 1)
        leave_sums(consume(1, ctx, vtl_ref[:, tk:tk + ctx], cmax, init))

    @pl.when(qi > n_latent_blocks)
    def _():
        finalize_previous()


def _attention(z, lams, cos, sin, nw, *, lambda_init, heads, seq, q_blk, k_blk, v_blk, g_blk):
    m = z.shape[0]
    ctx = m - seq
    tq = 256
    n_chunks = 4
    tk = seq // n_chunks
    assert seq % tq == 0 and ctx == tq, "one query block must hold exactly the context rows"
    assert seq % (n_chunks * LANES) == 0 and n_chunks % 2 == 0 and n_chunks >= 2
    nq = m // tq
    vrows = HEAD_DIM + 16
    lam_spec = pl.BlockSpec((1, HEAD_DIM // 2), lambda h, i: (0, 0))
    this_row = lambda base, per_head: pl.BlockSpec(
        (tq, HEAD_DIM), lambda h, i: (jnp.minimum(i, nq - 1), base + h * per_head))
    next_row = lambda base, per_head: pl.BlockSpec(
        (tq, HEAD_DIM), lambda h, i: (jnp.minimum(i + 1, nq - 1), base + h * per_head))
    prev_row = lambda base: pl.BlockSpec((tq, HEAD_DIM), lambda h, i: (jnp.maximum(i - 1, 0), base + h))
    kcol = lambda base: pl.BlockSpec((m, HEAD_DIM), lambda h, i: (0, base + h))
    table = pl.BlockSpec((m, HEAD_DIM), lambda h, i: (0, 0), pipeline_mode=pl.Buffered(1))
    return pl.pallas_call(
        functools.partial(_attention_kernel, lambda_init=lambda_init, seq=seq, tq=tq, tk=tk),
        grid=(heads, nq + 1),
        in_specs=[lam_spec, lam_spec, lam_spec, lam_spec,
                  this_row(q_blk, 1), this_row(0, 0), this_row(0, 0),
                  next_row(q_blk, 1), next_row(0, 0), next_row(0, 0),
                  kcol(k_blk), kcol(v_blk), table, table,
                  prev_row(g_blk),
                  pl.BlockSpec((1, HEAD_DIM), lambda h, i: (0, 0))],
        out_specs=prev_row(0),
        out_shape=jax.ShapeDtypeStruct((m, heads * HEAD_DIM), BF16),
        scratch_shapes=[pltpu.VMEM((m, HEAD_DIM), BF16),
                        pltpu.VMEM((n_chunks - 1, vrows, tk), BF16),
                        pltpu.VMEM((vrows, tk + ctx), BF16),
                        pltpu.VMEM((2, 2, tk + ctx, tq), F32),
                        pltpu.VMEM((2, 1, tq), F32),
                        pltpu.VMEM((2, vrows, tq), F32)],
        compiler_params=_params(("arbitrary", "arbitrary")),
        name="diff_attention",
    )(*lams, z, cos, sin, z, cos, sin, z, z, cos, sin, z, nw)


def _mlp_kernel(u_ref, v_ref, g_ref, vnw_ref, ws_ref, bst_ref, o_ref, *, groups):
    tr, width = u_ref.shape
    gd = width // groups
    v = _gelu(v_ref[...].astype(F32))
    v = (v * lax.rsqrt(jnp.mean(v * v, axis=-1, keepdims=True) + EPS) * vnw_ref[...]).astype(BF16)
    for c in range(tr // CHUNK):
        rows = slice(c * CHUNK, (c + 1) * CHUNK)
        for g in range(groups):
            cols = slice(g * gd, (g + 1) * gd)
            mixed = jnp.dot(ws_ref[g].astype(BF16), v[rows, cols], preferred_element_type=F32) + bst_ref[:, g:g + 1]
            u = _gelu(u_ref[rows, cols].astype(F32))
            o_ref[rows, cols] = (u * mixed * _silu(g_ref[rows, cols].astype(F32))).astype(o_ref.dtype)


def _chunk_mlp(z, vnw, ws, bst, *, u_blk, width):
    m = z.shape[0]
    groups = ws.shape[0]
    tr = 256
    col = lambda b: pl.BlockSpec((tr, width), lambda i: (i, b))
    return pl.pallas_call(
        functools.partial(_mlp_kernel, groups=groups),
        grid=(m // tr,),
        in_specs=[col(u_blk), col(u_blk + 1), col(u_blk + 2),
                  pl.BlockSpec((1, width), lambda i: (0, 0)),
                  pl.BlockSpec((groups, CHUNK, CHUNK), lambda i: (0, 0, 0)),
                  pl.BlockSpec((CHUNK, groups), lambda i: (0, 0))],
        out_specs=pl.BlockSpec((tr, width), lambda i: (i, 0)),
        out_shape=jax.ShapeDtypeStruct((m, width), BF16),
        compiler_params=_params(("arbitrary",)),
        name="chunk_mlp",
    )(z, z, z, vnw, ws, bst)


def _out_proj_kernel(ret_ref, dif_ref, mlp_ref, wr_ref, wd_ref, wm_ref, x_ref, gate_ref, o_ref, *, seq, tm):
    y = jnp.dot(ret_ref[...], wr_ref[...].astype(BF16), preferred_element_type=F32)
    y = y + jnp.dot(dif_ref[...], wd_ref[...].astype(BF16), preferred_element_type=F32)
    y = y + jnp.dot(mlp_ref[...], wm_ref[...].astype(BF16), preferred_element_type=F32)
    row = pl.program_id(0) * tm + lax.broadcasted_iota(jnp.int32, y.shape, 0)
    gate = jnp.where(row >= seq, gate_ref[1:2, :], gate_ref[0:1, :])
    o_ref[...] = x_ref[...] + gate * y


def _out_proj(ret, dif, mlp, w_out, layer, xx, mod, *, seq):
    m, d = xx.shape
    rw, dw, mw = ret.shape[1], dif.shape[1], mlp.shape[1]
    assert rw == dw and (rw + dw) % mw == 0
    tm = _pick(m, (1056, 768, 640, 256))
    tn = _pick(d, (512, 256))
    gate_blk = 2 * d // tn
    return pl.pallas_call(
        functools.partial(_out_proj_kernel, seq=seq, tm=tm),
        grid=(m // tm, d // tn),
        in_specs=[
            pl.BlockSpec((tm, rw), lambda i, j: (i, 0)),
            pl.BlockSpec((tm, dw), lambda i, j: (i, 0)),
            pl.BlockSpec((tm, mw), lambda i, j: (i, 0)),
            pl.BlockSpec((None, rw, tn), lambda i, j: (layer, 0, j)),
            pl.BlockSpec((None, dw, tn), lambda i, j: (layer, 1, j)),
            pl.BlockSpec((None, mw, tn), lambda i, j: (layer, (rw + dw) // mw, j)),
            pl.BlockSpec((tm, tn), lambda i, j: (i, j)),
            pl.BlockSpec((8, tn), lambda i, j: (0, gate_blk + j)),
        ],
        out_specs=pl.BlockSpec((tm, tn), lambda i, j: (i, j)),
        out_shape=jax.ShapeDtypeStruct((m, d), F32),
        compiler_params=_params(("arbitrary", "arbitrary")),
        name="out_proj",
    )(ret, dif, mlp, w_out, w_out, w_out, xx, mod)


def _final_norm_kernel(x_ref, w_ref, o_ref):
    x = x_ref[...]
    o_ref[...] = x * lax.rsqrt(jnp.mean(x * x, axis=-1, keepdims=True) + EPS) * w_ref[...]


def _final_norm(xx, w, seq):
    d = xx.shape[1]
    tr = 256
    return pl.pallas_call(
        _final_norm_kernel,
        grid=(seq // tr,),
        in_specs=[pl.BlockSpec((tr, d), lambda i: (i, 0)), pl.BlockSpec((1, d), lambda i: (0, 0))],
        out_specs=pl.BlockSpec((tr, d), lambda i: (i, 0)),
        out_shape=jax.ShapeDtypeStruct((seq, d), F32),
        compiler_params=_params(("arbitrary",)),
        name="final_norm",
    )(xx, w)


def _rope_tables(seq, ctx, dim, repeats):
    rows = seq // GRID_W
    n_freq = dim // 4
    inv = ROPE_BASE ** (-jnp.arange(n_freq, dtype=F32) / n_freq)
    ang_r = jnp.arange(rows, dtype=F32)[:, None] * inv
    ang_c = jnp.arange(GRID_W, dtype=F32)[:, None] * inv

    def per_token(f):
        by_row = jnp.broadcast_to(f(ang_r)[:, None, :], (rows, GRID_W, n_freq))
        by_col = jnp.broadcast_to(f(ang_c)[None, :, :], (rows, GRID_W, n_freq))
        return jnp.concatenate([by_row, by_col], axis=-1).reshape(seq, 2 * n_freq)

    cos_t, sin_t = per_token(jnp.cos), per_token(jnp.sin)
    cos = jnp.concatenate([cos_t, cos_t], axis=-1)
    sin = jnp.concatenate([-sin_t, sin_t], axis=-1)
    cos = jnp.concatenate([jnp.tile(cos, (1, repeats)), jnp.ones((ctx, repeats * dim), F32)], axis=0)
    sin = jnp.concatenate([jnp.tile(sin, (1, repeats)), jnp.zeros((ctx, repeats * dim), F32)], axis=0)
    return cos, sin


def kernel(x, c, ctx, c_ctx, w_in, w_out, w_ada, b_ada, norm_w, ret_decay_fwd, ret_decay_bwd,
           lam_q1, lam_k1, lam_q2, lam_k2, dif_norm_w, mlp_vnorm_w, mlp_ws, mlp_bs, final_norm_w):
    batch, seq, d = x.shape
    assert batch == 1, "the kernels process one sequence"
    n_ctx = ctx.shape[1]
    depth = w_in.shape[0]
    heads = ret_decay_fwd.shape[1]
    ret_w = heads * HEAD_DIM
    dif_w = ret_w
    mlp_w = mlp_vnorm_w.shape[1]
    assert 2 * ret_w + mlp_w == d and w_in.shape[2] == 8 * ret_w + 3 * mlp_w

    rk_blk, rv_blk, dk_blk, dv_blk = 0, heads, 2 * heads, 3 * heads
    rq_blk, dq_blk, rg_blk, dg_blk = 4 * heads, 5 * heads, 6 * heads, 7 * heads
    mu_blk = 8 * ret_w // mlp_w

    xx = (x[0], ctx[0])
    cc = jnp.zeros((8, d), F32).at[0].set(c[0]).at[1].set(c_ctx)
    mods = _modulation(cc, w_ada, b_ada)
    cos_r, sin_r = _rope_tables(seq, n_ctx, HEAD_DIM, 1)
    cos_d, sin_d = _rope_tables(seq, n_ctx, HEAD_DIM // 2, 2)

    for l in range(depth):
        lambda_init = 0.8 - 0.6 * math.exp(-0.3 * l)
        h = _norm_mod(xx, norm_w[l][None, :], mods[l], seq)
        if l == 0:
            h, xx = h
        z = _in_proj(h, w_in, l, BF16)
        ret = _retention(z, ret_decay_fwd, ret_decay_bwd, cos_r, sin_r, layer=l, heads=heads, seq=seq,
                         q_blk=rq_blk, k_blk=rk_blk, v_blk=rv_blk, g_blk=rg_blk)
        lams = [a[l][None, :] for a in (lam_q1, lam_k1, lam_q2, lam_k2)]
        dif = _attention(z, lams, cos_d, sin_d, dif_norm_w[l][None, :], lambda_init=lambda_init,
                         heads=heads, seq=seq, q_blk=dq_blk, k_blk=dk_blk, v_blk=dv_blk, g_blk=dg_blk)
        mlp = _chunk_mlp(z, mlp_vnorm_w[l][None, :], mlp_ws[l], mlp_bs[l].T, u_blk=mu_blk, width=mlp_w)
        xx = _out_proj(ret, dif, mlp, w_out, l, xx, mods[l], seq=seq)

    return _final_norm(xx, final_norm_w[None, :], seq)[None]
```

```python
import functools
import math

import jax
import jax.numpy as jnp
from jax import lax
from jax.experimental import pallas as pl
from jax.experimental.pallas import tpu as pltpu

HEAD_DIM = 128
CHUNK = 128
GRID_W = 64
ROPE_BASE = 10000.0
EPS = 1e-6
LOG2E = 1.4426950408889634
LANES = 128
NEG_BIG = -1e30
N_PARTS = 4
RET_UNROLL = 8
VMEM_LIMIT = 56 * 1024 * 1024

BF16 = jnp.bfloat16
F32 = jnp.float32

NT_DIMS = (((1,), (1,)), ((), ()))
TN_DIMS = (((0,), (0,)), ((), ()))


def _params(semantics, **kwargs):
    return pltpu.CompilerParams(dimension_semantics=semantics, vmem_limit_bytes=VMEM_LIMIT, **kwargs)


def _pick(n, candidates):
    for c in candidates:
        if n % c == 0:
            return c
    return n


def _silu(x):
    return x * (1.0 / (1.0 + jnp.exp(-x)))


def _gelu(x):
    return 0.5 * x * (1.0 + lax.erf(x * (2.0 ** -0.5)))


def _mod_kernel(cc_ref, w_ref, b_ref, o_ref):
    a = _silu(cc_ref[...]).astype(BF16)
    o_ref[...] = jnp.dot(a, w_ref[...].astype(BF16), preferred_element_type=F32) + b_ref[...]


def _modulation(cc, w_ada, b_ada):
    depth, d, n3 = w_ada.shape
    tn = _pick(n3, (512, 256, 128))
    return pl.pallas_call(
        _mod_kernel,
        grid=(depth, n3 // tn),
        in_specs=[
            pl.BlockSpec((8, d), lambda l, j: (0, 0)),
            pl.BlockSpec((None, d, tn), lambda l, j: (l, 0, j)),
            pl.BlockSpec((None, 1, tn), lambda l, j: (l, 0, j)),
        ],
        out_specs=pl.BlockSpec((None, 8, tn), lambda l, j: (l, 0, j)),
        out_shape=jax.ShapeDtypeStruct((depth, 8, n3), F32),
        compiler_params=_params(("arbitrary", "arbitrary")),
        name="modulation",
    )(cc, w_ada, b_ada.reshape(depth, 1, n3))


def _norm_mod_kernel(*refs, d, n_latent_blocks, joins_streams):
    is_ctx = pl.program_id(0) >= n_latent_blocks
    if joins_streams:
        x_ref, c_ref, nw_ref, mod_ref, h_ref, xx_ref = refs
        x = jnp.where(is_ctx, c_ref[...], x_ref[...])
        xx_ref[...] = x
    else:
        x_ref, nw_ref, mod_ref, h_ref = refs
        x = x_ref[...]
    y = x * lax.rsqrt(jnp.mean(x * x, axis=-1, keepdims=True) + EPS) * nw_ref[...]
    shift = jnp.where(is_ctx, mod_ref[1:2, 0:d], mod_ref[0:1, 0:d])
    scale = jnp.where(is_ctx, mod_ref[1:2, d:2 * d], mod_ref[0:1, d:2 * d])
    h_ref[...] = (y * (1.0 + scale) + shift).astype(h_ref.dtype)


def _norm_mod(streams, nw, mod, seq):
    joins_streams = isinstance(streams, tuple)
    tr = 256
    if joins_streams:
        x, c = streams
        d = x.shape[1]
        m = seq + c.shape[0]
        n_latent_blocks = seq // tr
        operands = (x, c)
        stream_specs = [pl.BlockSpec((tr, d), lambda i: (jnp.minimum(i, n_latent_blocks - 1), 0)),
                        pl.BlockSpec((tr, d), lambda i: (jnp.maximum(i - n_latent_blocks, 0), 0))]
    else:
        m, d = streams.shape
        operands = (streams,)
        stream_specs = [pl.BlockSpec((tr, d), lambda i: (i, 0))]
    row_block = pl.BlockSpec((tr, d), lambda i: (i, 0))
    out = pl.pallas_call(
        functools.partial(_norm_mod_kernel, d=d, n_latent_blocks=seq // tr, joins_streams=joins_streams),
        grid=(m // tr,),
        in_specs=stream_specs + [pl.BlockSpec((1, d), lambda i: (0, 0)),
                                 pl.BlockSpec((8, 3 * d), lambda i: (0, 0))],
        out_specs=(row_block, row_block) if joins_streams else row_block,
        out_shape=((jax.ShapeDtypeStruct((m, d), BF16), jax.ShapeDtypeStruct((m, d), F32))
                   if joins_streams else jax.ShapeDtypeStruct((m, d), BF16)),
        compiler_params=_params(("arbitrary",)),
        name="norm_mod",
    )(*operands, nw, mod)
    return out


def _matmul_kernel(a_ref, b_ref, o_ref):
    o_ref[...] = jnp.dot(a_ref[...], b_ref[...].astype(BF16), preferred_element_type=F32).astype(o_ref.dtype)


def _in_proj(h, w, layer, out_dtype):
    m, k = h.shape
    n = w.shape[2]
    tm = _pick(m, (1408, 1056, 768, 640, 256))
    tn = _pick(n, (512, 256))
    return pl.pallas_call(
        _matmul_kernel,
        grid=(m // tm, n // tn),
        in_specs=[
            pl.BlockSpec((tm, k), lambda i, j: (i, 0)),
            pl.BlockSpec((None, k, tn), lambda i, j: (layer, 0, j)),
        ],
        out_specs=pl.BlockSpec((tm, tn), lambda i, j: (i, j)),
        out_shape=jax.ShapeDtypeStruct((m, n), out_dtype),
        compiler_params=_params(("arbitrary", "arbitrary")),
        name="in_proj",
    )(h, w)


def _retention_kernel(decf_ref, decb_ref, q_ref, k_ref, v_ref, g_ref, cos_ref, sin_ref, o_ref,
                      sf_ref, sb_ref, kvb_ref, lhs_ref, *, layer, n_latent_chunks, n_ctx_chunks):
    head = pl.program_id(0)
    lgf = jnp.log1p(-jnp.exp(jnp.full((1, LANES), decf_ref[layer, head], F32)))
    lgb = jnp.log1p(-jnp.exp(jnp.full((1, LANES), decb_ref[layer, head], F32)))
    ri = lax.broadcasted_iota(jnp.int32, (CHUNK, CHUNK), 0)
    ci = lax.broadcasted_iota(jnp.int32, (CHUNK, CHUNK), 1)
    rel = (ri - ci).astype(F32)
    pos = ri.astype(F32)
    dmat = jnp.where(rel >= 0, jnp.exp(lgf * jnp.maximum(rel, 0.0)), jnp.exp(lgb * jnp.maximum(-rel, 0.0)))
    wfk = jnp.exp(lgf * (CHUNK - 1.0 - pos))
    wbk = jnp.exp(lgb * pos)
    dqf = jnp.exp(lgf * (pos + 1.0))
    dqb = jnp.exp(lgb * (CHUNK - pos))
    cdf = jnp.exp(lgf * float(CHUNK))
    cdb = jnp.exp(lgb * float(CHUNK))
    ret_scale = HEAD_DIM ** -0.5

    def rows_of(c):
        return pl.ds(pl.multiple_of(c * CHUNK, CHUNK), CHUNK)

    def rope(ref, rows):
        x = ref[rows, :].astype(F32)
        return x * cos_ref[rows, :] + pltpu.roll(x, HEAD_DIM // 2, 1) * sin_ref[rows, :]

    def scan_states(first, n, sf, sb):
        def up(t, sf):
            c = first + t
            rows = rows_of(c)
            qr = rope(q_ref, rows)
            kr = rope(k_ref, rows) * ret_scale
            a = lax.dot_general(qr.astype(BF16), kr.astype(BF16), NT_DIMS, preferred_element_type=F32) * dmat
            lhs_ref[c] = jnp.concatenate(
                [a.astype(BF16), (qr * dqf).astype(BF16), (qr * dqb).astype(BF16)], axis=1)
            kk = jnp.concatenate([(kr * wfk).astype(BF16), (kr * wbk).astype(BF16)], axis=1)
            kv = lax.dot_general(kk, v_ref[rows, :].astype(BF16), TN_DIMS, preferred_element_type=F32)
            sf_ref[c] = sf.astype(BF16)
            kvb_ref[c] = kv[HEAD_DIM:, :]
            return sf * cdf + kv[:HEAD_DIM, :]

        sf = lax.fori_loop(0, n, up, sf, unroll=min(n, RET_UNROLL))

        def down(t, sb):
            c = first + n - 1 - t
            sb_ref[c] = sb.astype(BF16)
            return sb * cdb + kvb_ref[c]

        sb = lax.fori_loop(0, n, down, sb, unroll=min(n, RET_UNROLL))
        return sf, sb

    def outputs(first, n):
        def one(t, carry):
            c = first + t
            rows = rows_of(c)
            rhs = jnp.concatenate([v_ref[rows, :].astype(BF16), sf_ref[c], sb_ref[c]], axis=0)
            o = jnp.dot(lhs_ref[c], rhs, preferred_element_type=F32)
            o = o * lax.rsqrt(jnp.mean(o * o, axis=-1, keepdims=True) + EPS)
            o_ref[rows, :] = (o * _silu(g_ref[rows, :].astype(F32))).astype(o_ref.dtype)
            return carry

        lax.fori_loop(0, n, one, 0, unroll=min(n, RET_UNROLL))

    zero = jnp.zeros((HEAD_DIM, HEAD_DIM), F32)
    sf, sb = scan_states(n_latent_chunks, n_ctx_chunks, zero, zero)
    scan_states(0, n_latent_chunks, sf, sb)
    outputs(0, n_latent_chunks)
    outputs(n_latent_chunks, n_ctx_chunks)


def _retention(z, dec_f, dec_b, cos, sin, *, layer, heads, seq, q_blk, k_blk, v_blk, g_blk):
    m = z.shape[0]
    col = lambda base: pl.BlockSpec((m, HEAD_DIM), lambda h: (0, base + h))
    smem = pl.BlockSpec(memory_space=pltpu.SMEM)
    full = pl.BlockSpec((m, HEAD_DIM), lambda h: (0, 0), pipeline_mode=pl.Buffered(1))
    return pl.pallas_call(
        functools.partial(_retention_kernel, layer=layer, n_latent_chunks=seq // CHUNK,
                          n_ctx_chunks=(m - seq) // CHUNK),
        grid=(heads,),
        in_specs=[smem, smem, col(q_blk), col(k_blk), col(v_blk), col(g_blk), full, full],
        out_specs=pl.BlockSpec((m, HEAD_DIM), lambda h: (0, h)),
        out_shape=jax.ShapeDtypeStruct((m, heads * HEAD_DIM), BF16),
        scratch_shapes=[pltpu.VMEM((m // CHUNK, HEAD_DIM, HEAD_DIM), BF16),
                        pltpu.VMEM((m // CHUNK, HEAD_DIM, HEAD_DIM), BF16),
                        pltpu.VMEM((m // CHUNK, HEAD_DIM, HEAD_DIM), F32),
                        pltpu.VMEM((m // CHUNK, CHUNK, 3 * HEAD_DIM), BF16)],
        compiler_params=_params(("arbitrary",)),
        name="retention",
    )(dec_f, dec_b, z, z, z, z, cos, sin)


def _swap_half_pairs(x):
    lane = lax.broadcasted_iota(jnp.int32, x.shape, 1)
    return jnp.where(lane % 64 < 32, pltpu.roll(x, 96, 1), pltpu.roll(x, 32, 1))


def _dif_rope(x, cos, sin):
    return x * cos + _swap_half_pairs(x) * sin


def _attention_kernel(lq1_ref, lk1_ref, lq2_ref, lk2_ref, q_ref, cosq_ref, sinq_ref, qn_ref, cosn_ref, sinn_ref,
                      k_ref, v_ref, cosk_ref, sink_ref, gp_ref, nw_ref, o_ref,
                      ks_ref, vtx_ref, vtl_ref, s0_ref, s1_ref, cm_ref, acc_ref, *, lambda_init, seq, tq, tk):
    qi = pl.program_id(1)
    m = k_ref.shape[0]
    ctx = m - seq
    vrows = vtl_ref.shape[0]
    n_chunks = seq // tk
    last = n_chunks - 1
    ones_rows = (lax.broadcasted_iota(jnp.int32, (vrows - HEAD_DIM, 1), 0) == 0).astype(BF16)
    slots = (s0_ref, s1_ref)

    def chunk_len(c):
        return tk + ctx if c == last else tk

    @pl.when(qi == 0)
    def _():
        def prep_rows(rows, n):
            k = _dif_rope(k_ref[rows, :].astype(F32), cosk_ref[rows, :], sink_ref[rows, :])
            ks_ref[rows, :] = k.astype(BF16)
            vt = v_ref[rows, :].astype(F32).T.astype(BF16)
            return jnp.concatenate([vt, jnp.broadcast_to(ones_rows, (vrows - HEAD_DIM, n))], axis=0)

        def prep(c, carry):
            vtx_ref[c] = prep_rows(pl.ds(pl.multiple_of(c * tk, tk), tk), tk)
            return carry
        lax.fori_loop(0, last, prep, 0)
        vtl_ref[:, 0:tk] = prep_rows(pl.ds(last * tk, tk), tk)
        vtl_ref[:, tk:tk + ctx] = prep_rows(pl.ds(seq, ctx), ctx)

    def query_operands(ref, cos_ref, sin_ref):
        q = _dif_rope(ref[...].astype(F32), cos_ref[...], sin_ref[...]) * ((HEAD_DIM // 2) ** -0.5 * LOG2E)
        lane = lax.broadcasted_iota(jnp.int32, q.shape, 1)
        q0 = jnp.where(lane < HEAD_DIM // 2, q, 0.0).astype(BF16)
        q1 = jnp.where(lane >= HEAD_DIM // 2, q, 0.0).astype(BF16)
        return q0, q1

    def scores(qs, row0, n, slot, off):
        kb = ks_ref[pl.ds(row0, n), :]
        cmax = []
        for comp, qm in enumerate(qs):
            s = lax.dot_general(kb, qm, NT_DIMS, preferred_element_type=F32)
            slots[slot][comp, off:off + n, :] = s
            cmax.append(jnp.max(s, axis=0, keepdims=True))
        return tuple(cmax)

    def rescale(cmax, state):
        new = []
        for comp in range(2):
            mx, acc = state[comp]
            mn = jnp.maximum(mx, cmax[comp])
            new.append((mn, jnp.exp2(mx - mn) * acc))
        return tuple(new)

    def accumulate(slot, off, n, vt, state):
        new = []
        for comp in range(2):
            mn, acc = state[comp]
            p = jnp.exp2(slots[slot][comp, off:off + n, :] - mn).astype(BF16)
            new.append((mn, acc + jnp.dot(vt, p, preferred_element_type=F32)))
        return tuple(new)

    def parts(c):
        step = tk // N_PARTS
        out = [(j * step, step) for j in range(N_PARTS)]
        if c == last:
            out[-1] = (out[-1][0], step + ctx)
        return out

    def vt_part(c, off, n):
        return vtl_ref[:, off:off + n] if c == last else vtx_ref[c, :, off:off + n]

    def finalize_previous():
        acc0 = acc_ref[0]
        acc1 = acc_ref[1]
        lam = (jnp.exp(jnp.sum(lq1_ref[...] * lk1_ref[...], axis=-1, keepdims=True))
               - jnp.exp(jnp.sum(lq2_ref[...] * lk2_ref[...], axis=-1, keepdims=True)) + lambda_init)
        l0 = acc0[HEAD_DIM:HEAD_DIM + 1, :]
        l1 = acc1[HEAD_DIM:HEAD_DIM + 1, :]
        ot = acc0[:HEAD_DIM, :] * (1.0 / l0) - lam * (acc1[:HEAD_DIM, :] * (1.0 / l1))
        o = ot.T
        o = o * lax.rsqrt(jnp.mean(o * o, axis=-1, keepdims=True) + EPS) * nw_ref[...] * (1.0 - lambda_init)
        o_ref[...] = (o * _silu(gp_ref[...].astype(F32))).astype(o_ref.dtype)

    def leave_sums(state):
        (_, acc0), (_, acc1) = state
        acc_ref[0] = acc0
        acc_ref[1] = acc1

    init = ((jnp.full((1, tq), NEG_BIG, F32), jnp.zeros((vrows, tq), F32)),) * 2
    qs = query_operands(q_ref, cosq_ref, sinq_ref)
    n_latent_blocks = seq // tq

    @pl.when(qi == 0)
    def _():
        cmax = scores(qs, 0, chunk_len(0), 0, 0)
        cm_ref[0] = cmax[0]
        cm_ref[1] = cmax[1]
        acc_ref[...] = jnp.ones(acc_ref.shape, F32)

    @pl.when(qi < n_latent_blocks)
    def _():
        finalize_previous()
        cmax = (cm_ref[0], cm_ref[1])
        state = init
        for c in range(n_chunks):
            state = rescale(cmax, state)
            nc = (c + 1) % n_chunks
            nqs = qs if c < last else query_operands(qn_ref, cosn_ref, sinn_ref)
            nxt = None
            for (off, n), (noff, nn) in zip(parts(c), parts(nc)):
                part = scores(nqs, nc * tk + noff, nn, nc % 2, noff)
                nxt = part if nxt is None else tuple(jnp.maximum(a, b) for a, b in zip(nxt, part))
                state = accumulate(c % 2, off, n, vt_part(c, off, n), state)
            cmax = nxt
        cm_ref[0] = cmax[0]
        cm_ref[1] = cmax[1]
        leave_sums(state)

    @pl.when(qi == n_latent_blocks)
    def _():
        finalize_previous()
        cmax = scores(qs, seq, ctx, 1, 0)
        leave_sums(accumulate(1, 0, ctx, vtl_ref[:, tk:tk + ctx], rescale(cmax, init)))

    @pl.when(qi > n_latent_blocks)
    def _():
        finalize_previous()


def _attention(z, lams, cos, sin, nw, *, lambda_init, heads, seq, q_blk, k_blk, v_blk, g_blk):
    m = z.shape[0]
    ctx = m - seq
    tq = 256
    n_chunks = 4
    tk = seq // n_chunks
    assert seq % tq == 0 and ctx == tq, "one query block must hold exactly the context rows"
    assert seq % (n_chunks * LANES) == 0 and n_chunks % 2 == 0 and n_chunks >= 2 and tk % N_PARTS == 0
    nq = m // tq
    vrows = HEAD_DIM + 16
    lam_spec = pl.BlockSpec((1, HEAD_DIM // 2), lambda h, i: (0, 0))
    this_row = lambda base, per_head: pl.BlockSpec(
        (tq, HEAD_DIM), lambda h, i: (jnp.minimum(i, nq - 1), base + h * per_head))
    next_row = lambda base, per_head: pl.BlockSpec(
        (tq, HEAD_DIM), lambda h, i: (jnp.minimum(i + 1, nq - 1), base + h * per_head))
    prev_row = lambda base: pl.BlockSpec((tq, HEAD_DIM), lambda h, i: (jnp.maximum(i - 1, 0), base + h))
    kcol = lambda base: pl.BlockSpec((m, HEAD_DIM), lambda h, i: (0, base + h))
    table = pl.BlockSpec((m, HEAD_DIM), lambda h, i: (0, 0), pipeline_mode=pl.Buffered(1))
    return pl.pallas_call(
        functools.partial(_attention_kernel, lambda_init=lambda_init, seq=seq, tq=tq, tk=tk),
        grid=(heads, nq + 1),
        in_specs=[lam_spec, lam_spec, lam_spec, lam_spec,
                  this_row(q_blk, 1), this_row(0, 0), this_row(0, 0),
                  next_row(q_blk, 1), next_row(0, 0), next_row(0, 0),
                  kcol(k_blk), kcol(v_blk), table, table,
                  prev_row(g_blk),
                  pl.BlockSpec((1, HEAD_DIM), lambda h, i: (0, 0))],
        out_specs=prev_row(0),
        out_shape=jax.ShapeDtypeStruct((m, heads * HEAD_DIM), BF16),
        scratch_shapes=[pltpu.VMEM((m, HEAD_DIM), BF16),
                        pltpu.VMEM((n_chunks - 1, vrows, tk), BF16),
                        pltpu.VMEM((vrows, tk + ctx), BF16),
                        pltpu.VMEM((2, tk + ctx, tq), F32),
                        pltpu.VMEM((2, tk + ctx, tq), F32),
                        pltpu.VMEM((2, 1, tq), F32),
                        pltpu.VMEM((2, vrows, tq), F32)],
        compiler_params=_params(("arbitrary", "arbitrary")),
        name="diff_attention",
    )(*lams, z, cos, sin, z, cos, sin, z, z, cos, sin, z, nw)


def _mlp_kernel(u_ref, v_ref, g_ref, vnw_ref, ws_ref, bst_ref, o_ref, *, groups):
    tr, width = u_ref.shape
    gd = width // groups
    v = _gelu(v_ref[...].astype(F32))
    v = (v * lax.rsqrt(jnp.mean(v * v, axis=-1, keepdims=True) + EPS) * vnw_ref[...]).astype(BF16)
    for c in range(tr // CHUNK):
        rows = slice(c * CHUNK, (c + 1) * CHUNK)
        for g in range(groups):
            cols = slice(g * gd, (g + 1) * gd)
            mixed = jnp.dot(ws_ref[g].astype(BF16), v[rows, cols], preferred_element_type=F32) + bst_ref[:, g:g + 1]
            u = _gelu(u_ref[rows, cols].astype(F32))
            o_ref[rows, cols] = (u * mixed * _silu(g_ref[rows, cols].astype(F32))).astype(o_ref.dtype)


def _chunk_mlp(z, vnw, ws, bst, *, u_blk, width):
    m = z.shape[0]
    groups = ws.shape[0]
    tr = 256
    col = lambda b: pl.BlockSpec((tr, width), lambda i: (i, b))
    return pl.pallas_call(
        functools.partial(_mlp_kernel, groups=groups),
        grid=(m // tr,),
        in_specs=[col(u_blk), col(u_blk + 1), col(u_blk + 2),
                  pl.BlockSpec((1, width), lambda i: (0, 0)),
                  pl.BlockSpec((groups, CHUNK, CHUNK), lambda i: (0, 0, 0)),
                  pl.BlockSpec((CHUNK, groups), lambda i: (0, 0))],
        out_specs=pl.BlockSpec((tr, width), lambda i: (i, 0)),
        out_shape=jax.ShapeDtypeStruct((m, width), BF16),
        compiler_params=_params(("arbitrary",)),
        name="chunk_mlp",
    )(z, z, z, vnw, ws, bst)


def _out_proj_kernel(ret_ref, dif_ref, mlp_ref, wr_ref, wd_ref, wm_ref, x_ref, gate_ref, o_ref, *, seq, tm):
    y = jnp.dot(ret_ref[...], wr_ref[...].astype(BF16), preferred_element_type=F32)
    y = y + jnp.dot(dif_ref[...], wd_ref[...].astype(BF16), preferred_element_type=F32)
    y = y + jnp.dot(mlp_ref[...], wm_ref[...].astype(BF16), preferred_element_type=F32)
    row = pl.program_id(0) * tm + lax.broadcasted_iota(jnp.int32, y.shape, 0)
    gate = jnp.where(row >= seq, gate_ref[1:2, :], gate_ref[0:1, :])
    o_ref[...] = x_ref[...] + gate * y


def _out_proj(ret, dif, mlp, w_out, layer, xx, mod, *, seq):
    m, d = xx.shape
    rw, dw, mw = ret.shape[1], dif.shape[1], mlp.shape[1]
    assert rw == dw and (rw + dw) % mw == 0
    tm = _pick(m, (1056, 768, 640, 256))
    tn = _pick(d, (512, 256))
    gate_blk = 2 * d // tn
    return pl.pallas_call(
        functools.partial(_out_proj_kernel, seq=seq, tm=tm),
        grid=(m // tm, d // tn),
        in_specs=[
            pl.BlockSpec((tm, rw), lambda i, j: (i, 0)),
            pl.BlockSpec((tm, dw), lambda i, j: (i, 0)),
            pl.BlockSpec((tm, mw), lambda i, j: (i, 0)),
            pl.BlockSpec((None, rw, tn), lambda i, j: (layer, 0, j)),
            pl.BlockSpec((None, dw, tn), lambda i, j: (layer, 1, j)),
            pl.BlockSpec((None, mw, tn), lambda i, j: (layer, (rw + dw) // mw, j)),
            pl.BlockSpec((tm, tn), lambda i, j: (i, j)),
            pl.BlockSpec((8, tn), lambda i, j: (0, gate_blk + j)),
        ],
        out_specs=pl.BlockSpec((tm, tn), lambda i, j: (i, j)),
        out_shape=jax.ShapeDtypeStruct((m, d), F32),
        compiler_params=_params(("arbitrary", "arbitrary")),
        name="out_proj",
    )(ret, dif, mlp, w_out, w_out, w_out, xx, mod)


def _final_norm_kernel(x_ref, w_ref, o_ref):
    x = x_ref[...]
    o_ref[...] = x * lax.rsqrt(jnp.mean(x * x, axis=-1, keepdims=True) + EPS) * w_ref[...]


def _final_norm(xx, w, seq):
    d = xx.shape[1]
    tr = 256
    return pl.pallas_call(
        _final_norm_kernel,
        grid=(seq // tr,),
        in_specs=[pl.BlockSpec((tr, d), lambda i: (i, 0)), pl.BlockSpec((1, d), lambda i: (0, 0))],
        out_specs=pl.BlockSpec((tr, d), lambda i: (i, 0)),
        out_shape=jax.ShapeDtypeStruct((seq, d), F32),
        compiler_params=_params(("arbitrary",)),
        name="final_norm",
    )(xx, w)


def _rope_tables(seq, ctx, dim, repeats):
    rows = seq // GRID_W
    n_freq = dim // 4
    inv = ROPE_BASE ** (-jnp.arange(n_freq, dtype=F32) / n_freq)
    ang_r = jnp.arange(rows, dtype=F32)[:, None] * inv
    ang_c = jnp.arange(GRID_W, dtype=F32)[:, None] * inv

    def per_token(f):
        by_row = jnp.broadcast_to(f(ang_r)[:, None, :], (rows, GRID_W, n_freq))
        by_col = jnp.broadcast_to(f(ang_c)[None, :, :], (rows, GRID_W, n_freq))
        return jnp.concatenate([by_row, by_col], axis=-1).reshape(seq, 2 * n_freq)

    cos_t, sin_t = per_token(jnp.cos), per_token(jnp.sin)
    cos = jnp.concatenate([cos_t, cos_t], axis=-1)
    sin = jnp.concatenate([-sin_t, sin_t], axis=-1)
    cos = jnp.concatenate([jnp.tile(cos, (1, repeats)), jnp.ones((ctx, repeats * dim), F32)], axis=0)
    sin = jnp.concatenate([jnp.tile(sin, (1, repeats)), jnp.zeros((ctx, repeats * dim), F32)], axis=0)
    return cos, sin


def kernel(x, c, ctx, c_ctx, w_in, w_out, w_ada, b_ada, norm_w, ret_decay_fwd, ret_decay_bwd,
           lam_q1, lam_k1, lam_q2, lam_k2, dif_norm_w, mlp_vnorm_w, mlp_ws, mlp_bs, final_norm_w):
    batch, seq, d = x.shape
    assert batch == 1, "the kernels process one sequence"
    n_ctx = ctx.shape[1]
    depth = w_in.shape[0]
    heads = ret_decay_fwd.shape[1]
    ret_w = heads * HEAD_DIM
    dif_w = ret_w
    mlp_w = mlp_vnorm_w.shape[1]
    assert 2 * ret_w + mlp_w == d and w_in.shape[2] == 8 * ret_w + 3 * mlp_w

    rk_blk, rv_blk, dk_blk, dv_blk = 0, heads, 2 * heads, 3 * heads
    rq_blk, dq_blk, rg_blk, dg_blk = 4 * heads, 5 * heads, 6 * heads, 7 * heads
    mu_blk = 8 * ret_w // mlp_w

    xx = (x[0], ctx[0])
    cc = jnp.zeros((8, d), F32).at[0].set(c[0]).at[1].set(c_ctx)
    mods = _modulation(cc, w_ada, b_ada)
    cos_r, sin_r = _rope_tables(seq, n_ctx, HEAD_DIM, 1)
    cos_d, sin_d = _rope_tables(seq, n_ctx, HEAD_DIM // 2, 2)

    for l in range(depth):
        lambda_init = 0.8 - 0.6 * math.exp(-0.3 * l)
        h = _norm_mod(xx, norm_w[l][None, :], mods[l], seq)
        if l == 0:
            h, xx = h
        z = _in_proj(h, w_in, l, BF16)
        ret = _retention(z, ret_decay_fwd, ret_decay_bwd, cos_r, sin_r, layer=l, heads=heads, seq=seq,
                         q_blk=rq_blk, k_blk=rk_blk, v_blk=rv_blk, g_blk=rg_blk)
        lams = [a[l][None, :] for a in (lam_q1, lam_k1, lam_q2, lam_k2)]
        dif = _attention(z, lams, cos_d, sin_d, dif_norm_w[l][None, :], lambda_init=lambda_init,
                         heads=heads, seq=seq, q_blk=dq_blk, k_blk=dk_blk, v_blk=dv_blk, g_blk=dg_blk)
        mlp = _chunk_mlp(z, mlp_vnorm_w[l][None, :], mlp_ws[l], mlp_bs[l].T, u_blk=mu_blk, width=mlp_w)
        xx = _out_proj(ret, dif, mlp, w_out, l, xx, mods[l], seq=seq)

    return _final_norm(xx, final_norm_w[None, :], seq)[None]
```

```python
import functools
import math

import jax
import jax.numpy as jnp
from jax import lax
from jax.experimental import pallas as pl
from jax.experimental.pallas import tpu as pltpu

HEAD_DIM = 128
CHUNK = 128
GRID_W = 64
ROPE_BASE = 10000.0
EPS = 1e-6
LOG2E = 1.4426950408889634
LANES = 128
NEG_BIG = -1e30
N_PARTS = 4
RET_UNROLL = 8
VMEM_LIMIT = 56 * 1024 * 1024

BF16 = jnp.bfloat16
F32 = jnp.float32

NT_DIMS = (((1,), (1,)), ((), ()))
TN_DIMS = (((0,), (0,)), ((), ()))


def _params(semantics, **kwargs):
    return pltpu.CompilerParams(dimension_semantics=semantics, vmem_limit_bytes=VMEM_LIMIT, **kwargs)


def _pick(n, candidates):
    for c in candidates:
        if n % c == 0:
            return c
    return n


def _silu(x):
    return x * (1.0 / (1.0 + jnp.exp(-x)))


def _gelu(x):
    return 0.5 * x * (1.0 + lax.erf(x * (2.0 ** -0.5)))


def _mod_kernel(cc_ref, w_ref, b_ref, o_ref):
    a = _silu(cc_ref[...]).astype(BF16)
    o_ref[...] = jnp.dot(a, w_ref[...].astype(BF16), preferred_element_type=F32) + b_ref[...]


def _modulation(cc, w_ada, b_ada):
    depth, d, n3 = w_ada.shape
    tn = _pick(n3, (512, 256, 128))
    return pl.pallas_call(
        _mod_kernel,
        grid=(depth, n3 // tn),
        in_specs=[
            pl.BlockSpec((8, d), lambda l, j: (0, 0)),
            pl.BlockSpec((None, d, tn), lambda l, j: (l, 0, j)),
            pl.BlockSpec((None, 1, tn), lambda l, j: (l, 0, j)),
        ],
        out_specs=pl.BlockSpec((None, 8, tn), lambda l, j: (l, 0, j)),
        out_shape=jax.ShapeDtypeStruct((depth, 8, n3), F32),
        compiler_params=_params(("arbitrary", "arbitrary")),
        name="modulation",
    )(cc, w_ada, b_ada.reshape(depth, 1, n3))


def _norm_mod_kernel(*refs, d, n_latent_blocks, joins_streams):
    is_ctx = pl.program_id(0) >= n_latent_blocks
    if joins_streams:
        x_ref, c_ref, nw_ref, mod_ref, h_ref, xx_ref = refs
        x = jnp.where(is_ctx, c_ref[...], x_ref[...])
        xx_ref[...] = x
    else:
        x_ref, nw_ref, mod_ref, h_ref = refs
        x = x_ref[...]
    y = x * lax.rsqrt(jnp.mean(x * x, axis=-1, keepdims=True) + EPS) * nw_ref[...]
    shift = jnp.where(is_ctx, mod_ref[1:2, 0:d], mod_ref[0:1, 0:d])
    scale = jnp.where(is_ctx, mod_ref[1:2, d:2 * d], mod_ref[0:1, d:2 * d])
    h_ref[...] = (y * (1.0 + scale) + shift).astype(h_ref.dtype)


def _norm_mod(streams, nw, mod, seq):
    joins_streams = isinstance(streams, tuple)
    tr = 256
    if joins_streams:
        x, c = streams
        d = x.shape[1]
        m = seq + c.shape[0]
        n_latent_blocks = seq // tr
        operands = (x, c)
        stream_specs = [pl.BlockSpec((tr, d), lambda i: (jnp.minimum(i, n_latent_blocks - 1), 0)),
                        pl.BlockSpec((tr, d), lambda i: (jnp.maximum(i - n_latent_blocks, 0), 0))]
    else:
        m, d = streams.shape
        operands = (streams,)
        stream_specs = [pl.BlockSpec((tr, d), lambda i: (i, 0))]
    row_block = pl.BlockSpec((tr, d), lambda i: (i, 0))
    out = pl.pallas_call(
        functools.partial(_norm_mod_kernel, d=d, n_latent_blocks=seq // tr, joins_streams=joins_streams),
        grid=(m // tr,),
        in_specs=stream_specs + [pl.BlockSpec((1, d), lambda i: (0, 0)),
                                 pl.BlockSpec((8, 3 * d), lambda i: (0, 0))],
        out_specs=(row_block, row_block) if joins_streams else row_block,
        out_shape=((jax.ShapeDtypeStruct((m, d), BF16), jax.ShapeDtypeStruct((m, d), F32))
                   if joins_streams else jax.ShapeDtypeStruct((m, d), BF16)),
        compiler_params=_params(("arbitrary",)),
        name="norm_mod",
    )(*operands, nw, mod)
    return out


def _matmul_kernel(a_ref, b_ref, o_ref):
    o_ref[...] = jnp.dot(a_ref[...], b_ref[...].astype(BF16), preferred_element_type=F32).astype(o_ref.dtype)


def _in_proj(h, w, layer, out_dtype):
    m, k = h.shape
    n = w.shape[2]
    tm = _pick(m, (1408, 1056, 768, 640, 256))
    tn = _pick(n, (512, 256))
    return pl.pallas_call(
        _matmul_kernel,
        grid=(m // tm, n // tn),
        in_specs=[
            pl.BlockSpec((tm, k), lambda i, j: (i, 0)),
            pl.BlockSpec((None, k, tn), lambda i, j: (layer, 0, j)),
        ],
        out_specs=pl.BlockSpec((tm, tn), lambda i, j: (i, j)),
        out_shape=jax.ShapeDtypeStruct((m, n), out_dtype),
        compiler_params=_params(("arbitrary", "arbitrary")),
        name="in_proj",
    )(h, w)


def _retention_kernel(decf_ref, decb_ref, q_ref, k_ref, v_ref, g_ref, cos_ref, sin_ref, o_ref,
                      sf_ref, sb_ref, kvb_ref, lhs_ref, *, layer, n_latent_chunks, n_ctx_chunks):
    head = pl.program_id(0)
    lgf = jnp.log1p(-jnp.exp(jnp.full((1, LANES), decf_ref[layer, head], F32)))
    lgb = jnp.log1p(-jnp.exp(jnp.full((1, LANES), decb_ref[layer, head], F32)))
    ri = lax.broadcasted_iota(jnp.int32, (CHUNK, CHUNK), 0)
    ci = lax.broadcasted_iota(jnp.int32, (CHUNK, CHUNK), 1)
    rel = (ri - ci).astype(F32)
    pos = ri.astype(F32)
    dmat = jnp.where(rel >= 0, jnp.exp(lgf * jnp.maximum(rel, 0.0)), jnp.exp(lgb * jnp.maximum(-rel, 0.0)))
    wfk = jnp.exp(lgf * (CHUNK - 1.0 - pos))
    wbk = jnp.exp(lgb * pos)
    dqf = jnp.exp(lgf * (pos + 1.0))
    dqb = jnp.exp(lgb * (CHUNK - pos))
    cdf = jnp.exp(lgf * float(CHUNK))
    cdb = jnp.exp(lgb * float(CHUNK))
    ret_scale = HEAD_DIM ** -0.5

    def rows_of(c):
        return pl.ds(pl.multiple_of(c * CHUNK, CHUNK), CHUNK)

    def rope(ref, rows):
        x = ref[rows, :].astype(F32)
        return x * cos_ref[rows, :] + pltpu.roll(x, HEAD_DIM // 2, 1) * sin_ref[rows, :]

    def scan_states(first, n, sf, sb):
        def up(t, sf):
            c = first + t
            rows = rows_of(c)
            qr = rope(q_ref, rows)
            kr = rope(k_ref, rows) * ret_scale
            a = lax.dot_general(qr.astype(BF16), kr.astype(BF16), NT_DIMS, preferred_element_type=F32) * dmat
            lhs_ref[c] = jnp.concatenate(
                [a.astype(BF16), (qr * dqf).astype(BF16), (qr * dqb).astype(BF16)], axis=1)
            kk = jnp.concatenate([(kr * wfk).astype(BF16), (kr * wbk).astype(BF16)], axis=1)
            kv = lax.dot_general(kk, v_ref[rows, :].astype(BF16), TN_DIMS, preferred_element_type=F32)
            sf_ref[c] = sf.astype(BF16)
            kvb_ref[c] = kv[HEAD_DIM:, :]
            return sf * cdf + kv[:HEAD_DIM, :]

        sf = lax.fori_loop(0, n, up, sf, unroll=min(n, RET_UNROLL))

        def down(t, sb):
            c = first + n - 1 - t
            sb_ref[c] = sb.astype(BF16)
            return sb * cdb + kvb_ref[c]

        sb = lax.fori_loop(0, n, down, sb, unroll=min(n, RET_UNROLL))
        return sf, sb

    def outputs(first, n):
        def one(t, carry):
            c = first + t
            rows = rows_of(c)
            rhs = jnp.concatenate([v_ref[rows, :].astype(BF16), sf_ref[c], sb_ref[c]], axis=0)
            o = jnp.dot(lhs_ref[c], rhs, preferred_element_type=F32)
            o = o * lax.rsqrt(jnp.mean(o * o, axis=-1, keepdims=True) + EPS)
            o_ref[rows, :] = (o * _silu(g_ref[rows, :].astype(F32))).astype(o_ref.dtype)
            return carry

        lax.fori_loop(0, n, one, 0, unroll=min(n, RET_UNROLL))

    zero = jnp.zeros((HEAD_DIM, HEAD_DIM), F32)
    sf, sb = scan_states(n_latent_chunks, n_ctx_chunks, zero, zero)
    scan_states(0, n_latent_chunks, sf, sb)
    outputs(0, n_latent_chunks)
    outputs(n_latent_chunks, n_ctx_chunks)


def _retention(z, dec_f, dec_b, cos, sin, *, layer, heads, seq, q_blk, k_blk, v_blk, g_blk):
    m = z.shape[0]
    col = lambda base: pl.BlockSpec((m, HEAD_DIM), lambda h: (0, base + h))
    smem = pl.BlockSpec(memory_space=pltpu.SMEM)
    full = pl.BlockSpec((m, HEAD_DIM), lambda h: (0, 0), pipeline_mode=pl.Buffered(1))
    return pl.pallas_call(
        functools.partial(_retention_kernel, layer=layer, n_latent_chunks=seq // CHUNK,
                          n_ctx_chunks=(m - seq) // CHUNK),
        grid=(heads,),
        in_specs=[smem, smem, col(q_blk), col(k_blk), col(v_blk), col(g_blk), full, full],
        out_specs=pl.BlockSpec((m, HEAD_DIM), lambda h: (0, h)),
        out_shape=jax.ShapeDtypeStruct((m, heads * HEAD_DIM), BF16),
        scratch_shapes=[pltpu.VMEM((m // CHUNK, HEAD_DIM, HEAD_DIM), BF16),
                        pltpu.VMEM((m // CHUNK, HEAD_DIM, HEAD_DIM), BF16),
                        pltpu.VMEM((m // CHUNK, HEAD_DIM, HEAD_DIM), F32),
                        pltpu.VMEM((m // CHUNK, CHUNK, 3 * HEAD_DIM), BF16)],
        compiler_params=_params(("arbitrary",)),
        name="retention",
    )(dec_f, dec_b, z, z, z, z, cos, sin)


def _swap_half_pairs(x):
    lane = lax.broadcasted_iota(jnp.int32, x.shape, 1)
    return jnp.where(lane % 64 < 32, pltpu.roll(x, 96, 1), pltpu.roll(x, 32, 1))


def _dif_rope(x, cos, sin):
    return x * cos + _swap_half_pairs(x) * sin


def _attention_kernel(lq1_ref, lk1_ref, lq2_ref, lk2_ref, q_ref, cosq_ref, sinq_ref, qn_ref, cosn_ref, sinn_ref,
                      k_ref, v_ref, cosk_ref, sink_ref, gp_ref, nw_ref, o_ref,
                      ks_ref, vtx_ref, vtl_ref, s0_ref, s1_ref, cm_ref, acc_ref, *, lambda_init, seq, tq, tk):
    qi = pl.program_id(1)
    m = k_ref.shape[0]
    ctx = m - seq
    vrows = vtl_ref.shape[0]
    n_chunks = seq // tk
    last = n_chunks - 1
    ones_rows = (lax.broadcasted_iota(jnp.int32, (vrows - HEAD_DIM, 1), 0) == 0).astype(BF16)
    slots = (s0_ref, s1_ref)

    def chunk_len(c):
        return tk + ctx if c == last else tk

    @pl.when(qi == 0)
    def _():
        def prep_rows(rows, n):
            k = _dif_rope(k_ref[rows, :].astype(F32), cosk_ref[rows, :], sink_ref[rows, :])
            ks_ref[rows, :] = k.astype(BF16)
            vt = v_ref[rows, :].astype(F32).T.astype(BF16)
            return jnp.concatenate([vt, jnp.broadcast_to(ones_rows, (vrows - HEAD_DIM, n))], axis=0)

        def prep(c, carry):
            vtx_ref[c] = prep_rows(pl.ds(pl.multiple_of(c * tk, tk), tk), tk)
            return carry
        lax.fori_loop(0, last, prep, 0)
        vtl_ref[:, 0:tk] = prep_rows(pl.ds(last * tk, tk), tk)
        vtl_ref[:, tk:tk + ctx] = prep_rows(pl.ds(seq, ctx), ctx)

    def query_operands(ref, cos_ref, sin_ref):
        q = _dif_rope(ref[...].astype(F32), cos_ref[...], sin_ref[...]) * ((HEAD_DIM // 2) ** -0.5 * LOG2E)
        lane = lax.broadcasted_iota(jnp.int32, q.shape, 1)
        q0 = jnp.where(lane < HEAD_DIM // 2, q, 0.0).T.astype(BF16)
        q1 = jnp.where(lane >= HEAD_DIM // 2, q, 0.0).T.astype(BF16)
        return q0, q1

    def scores(qs, row0, n, slot, off):
        kb = ks_ref[pl.ds(row0, n), :]
        cmax = []
        for comp, qm in enumerate(qs):
            s = jnp.dot(kb, qm, preferred_element_type=F32)
            slots[slot][comp, off:off + n, :] = s
            cmax.append(jnp.max(s, axis=0, keepdims=True))
        return tuple(cmax)

    def rescale(cmax, state):
        new = []
        for comp in range(2):
            mx, acc = state[comp]
            mn = jnp.maximum(mx, cmax[comp])
            new.append((mn, jnp.exp2(mx - mn) * acc))
        return tuple(new)

    def accumulate(slot, off, n, vt, state):
        new = []
        for comp in range(2):
            mn, acc = state[comp]
            p = jnp.exp2(slots[slot][comp, off:off + n, :] - mn).astype(BF16)
            new.append((mn, acc + jnp.dot(vt, p, preferred_element_type=F32)))
        return tuple(new)

    def parts(c):
        step = tk // N_PARTS
        out = [(j * step, step) for j in range(N_PARTS)]
        if c == last:
            out[-1] = (out[-1][0], step + ctx)
        return out

    def vt_part(c, off, n):
        return vtl_ref[:, off:off + n] if c == last else vtx_ref[c, :, off:off + n]

    def finalize_previous():
        acc0 = acc_ref[0]
        acc1 = acc_ref[1]
        lam = (jnp.exp(jnp.sum(lq1_ref[...] * lk1_ref[...], axis=-1, keepdims=True))
               - jnp.exp(jnp.sum(lq2_ref[...] * lk2_ref[...], axis=-1, keepdims=True)) + lambda_init)
        l0 = acc0[HEAD_DIM:HEAD_DIM + 1, :]
        l1 = acc1[HEAD_DIM:HEAD_DIM + 1, :]
        ot = acc0[:HEAD_DIM, :] * (1.0 / l0) - lam * (acc1[:HEAD_DIM, :] * (1.0 / l1))
        o = ot.T
        o = o * lax.rsqrt(jnp.mean(o * o, axis=-1, keepdims=True) + EPS) * nw_ref[...] * (1.0 - lambda_init)
        o_ref[...] = (o * _silu(gp_ref[...].astype(F32))).astype(o_ref.dtype)

    def leave_sums(state):
        (_, acc0), (_, acc1) = state
        acc_ref[0] = acc0
        acc_ref[1] = acc1

    init = ((jnp.full((1, tq), NEG_BIG, F32), jnp.zeros((vrows, tq), F32)),) * 2
    qs = query_operands(q_ref, cosq_ref, sinq_ref)
    n_latent_blocks = seq // tq

    @pl.when(qi == 0)
    def _():
        cmax = scores(qs, 0, chunk_len(0), 0, 0)
        cm_ref[0] = cmax[0]
        cm_ref[1] = cmax[1]
        acc_ref[...] = jnp.ones(acc_ref.shape, F32)

    @pl.when(qi < n_latent_blocks)
    def _():
        finalize_previous()
        cmax = (cm_ref[0], cm_ref[1])
        state = init
        for c in range(n_chunks):
            state = rescale(cmax, state)
            nc = (c + 1) % n_chunks
            nqs = qs if c < last else query_operands(qn_ref, cosn_ref, sinn_ref)
            nxt = None
            for (off, n), (noff, nn) in zip(parts(c), parts(nc)):
                part = scores(nqs, nc * tk + noff, nn, nc % 2, noff)
                nxt = part if nxt is None else tuple(jnp.maximum(a, b) for a, b in zip(nxt, part))
                state = accumulate(c % 2, off, n, vt_part(c, off, n), state)
            cmax = nxt
        cm_ref[0] = cmax[0]
        cm_ref[1] = cmax[1]
        leave_sums(state)

    @pl.when(qi == n_latent_blocks)
    def _():
        finalize_previous()
        cmax = scores(qs, seq, ctx, 1, 0)
        leave_sums(accumulate(1, 0, ctx, vtl_ref[:, tk:tk + ctx], rescale(cmax, init)))

    @pl.when(qi > n_latent_blocks)
    def _():
        finalize_previous()


def _attention(z, lams, cos, sin, nw, *, lambda_init, heads, seq, q_blk, k_blk, v_blk, g_blk):
    m = z.shape[0]
    ctx = m - seq
    tq = 256
    n_chunks = 4
    tk = seq // n_chunks
    assert seq % tq == 0 and ctx == tq, "one query block must hold exactly the context rows"
    assert seq % (n_chunks * LANES) == 0 and n_chunks % 2 == 0 and n_chunks >= 2 and tk % N_PARTS == 0
    nq = m // tq
    vrows = HEAD_DIM + 16
    lam_spec = pl.BlockSpec((1, HEAD_DIM // 2), lambda h, i: (0, 0))
    this_row = lambda base, per_head: pl.BlockSpec(
        (tq, HEAD_DIM), lambda h, i: (jnp.minimum(i, nq - 1), base + h * per_head))
    next_row = lambda base, per_head: pl.BlockSpec(
        (tq, HEAD_DIM), lambda h, i: (jnp.minimum(i + 1, nq - 1), base + h * per_head))
    prev_row = lambda base: pl.BlockSpec((tq, HEAD_DIM), lambda h, i: (jnp.maximum(i - 1, 0), base + h))
    kcol = lambda base: pl.BlockSpec((m, HEAD_DIM), lambda h, i: (0, base + h))
    table = pl.BlockSpec((m, HEAD_DIM), lambda h, i: (0, 0), pipeline_mode=pl.Buffered(1))
    return pl.pallas_call(
        functools.partial(_attention_kernel, lambda_init=lambda_init, seq=seq, tq=tq, tk=tk),
        grid=(heads, nq + 1),
        in_specs=[lam_spec, lam_spec, lam_spec, lam_spec,
                  this_row(q_blk, 1), this_row(0, 0), this_row(0, 0),
                  next_row(q_blk, 1), next_row(0, 0), next_row(0, 0),
                  kcol(k_blk), kcol(v_blk), table, table,
                  prev_row(g_blk),
                  pl.BlockSpec((1, HEAD_DIM), lambda h, i: (0, 0))],
        out_specs=prev_row(0),
        out_shape=jax.ShapeDtypeStruct((m, heads * HEAD_DIM), BF16),
        scratch_shapes=[pltpu.VMEM((m, HEAD_DIM), BF16),
                        pltpu.VMEM((n_chunks - 1, vrows, tk), BF16),
                        pltpu.VMEM((vrows, tk + ctx), BF16),
                        pltpu.VMEM((2, tk + ctx, tq), F32),
                        pltpu.VMEM((2, tk + ctx, tq), F32),
                        pltpu.VMEM((2, 1, tq), F32),
                        pltpu.VMEM((2, vrows, tq), F32)],
        compiler_params=_params(("arbitrary", "arbitrary")),
        name="diff_attention",
    )(*lams, z, cos, sin, z, cos, sin, z, z, cos, sin, z, nw)


def _mlp_kernel(u_ref, v_ref, g_ref, vnw_ref, ws_ref, bst_ref, o_ref, *, groups):
    tr, width = u_ref.shape
    gd = width // groups
    v = _gelu(v_ref[...].astype(F32))
    v = (v * lax.rsqrt(jnp.mean(v * v, axis=-1, keepdims=True) + EPS) * vnw_ref[...]).astype(BF16)
    for c in range(tr // CHUNK):
        rows = slice(c * CHUNK, (c + 1) * CHUNK)
        for g in range(groups):
            cols = slice(g * gd, (g + 1) * gd)
            mixed = jnp.dot(ws_ref[g].astype(BF16), v[rows, cols], preferred_element_type=F32) + bst_ref[:, g:g + 1]
            u = _gelu(u_ref[rows, cols].astype(F32))
            o_ref[rows, cols] = (u * mixed * _silu(g_ref[rows, cols].astype(F32))).astype(o_ref.dtype)


def _chunk_mlp(z, vnw, ws, bst, *, u_blk, width):
    m = z.shape[0]
    groups = ws.shape[0]
    tr = 256
    col = lambda b: pl.BlockSpec((tr, width), lambda i: (i, b))
    return pl.pallas_call(
        functools.partial(_mlp_kernel, groups=groups),
        grid=(m // tr,),
        in_specs=[col(u_blk), col(u_blk + 1), col(u_blk + 2),
                  pl.BlockSpec((1, width), lambda i: (0, 0)),
                  pl.BlockSpec((groups, CHUNK, CHUNK), lambda i: (0, 0, 0)),
                  pl.BlockSpec((CHUNK, groups), lambda i: (0, 0))],
        out_specs=pl.BlockSpec((tr, width), lambda i: (i, 0)),
        out_shape=jax.ShapeDtypeStruct((m, width), BF16),
        compiler_params=_params(("arbitrary",)),
        name="chunk_mlp",
    )(z, z, z, vnw, ws, bst)


def _out_proj_kernel(ret_ref, dif_ref, mlp_ref, wr_ref, wd_ref, wm_ref, x_ref, gate_ref, o_ref, *, seq, tm):
    y = jnp.dot(ret_ref[...], wr_ref[...].astype(BF16), preferred_element_type=F32)
    y = y + jnp.dot(dif_ref[...], wd_ref[...].astype(BF16), preferred_element_type=F32)
    y = y + jnp.dot(mlp_ref[...], wm_ref[...].astype(BF16), preferred_element_type=F32)
    row = pl.program_id(0) * tm + lax.broadcasted_iota(jnp.int32, y.shape, 0)
    gate = jnp.where(row >= seq, gate_ref[1:2, :], gate_ref[0:1, :])
    o_ref[...] = x_ref[...] + gate * y


def _out_proj(ret, dif, mlp, w_out, layer, xx, mod, *, seq):
    m, d = xx.shape
    rw, dw, mw = ret.shape[1], dif.shape[1], mlp.shape[1]
    assert rw == dw and (rw + dw) % mw == 0
    tm = _pick(m, (1056, 768, 640, 256))
    tn = _pick(d, (512, 256))
    gate_blk = 2 * d // tn
    return pl.pallas_call(
        functools.partial(_out_proj_kernel, seq=seq, tm=tm),
        grid=(m // tm, d // tn),
        in_specs=[
            pl.BlockSpec((tm, rw), lambda i, j: (i, 0)),
            pl.BlockSpec((tm, dw), lambda i, j: (i, 0)),
            pl.BlockSpec((tm, mw), lambda i, j: (i, 0)),
            pl.BlockSpec((None, rw, tn), lambda i, j: (layer, 0, j)),
            pl.BlockSpec((None, dw, tn), lambda i, j: (layer, 1, j)),
            pl.BlockSpec((None, mw, tn), lambda i, j: (layer, (rw + dw) // mw, j)),
            pl.BlockSpec((tm, tn), lambda i, j: (i, j)),
            pl.BlockSpec((8, tn), lambda i, j: (0, gate_blk + j)),
        ],
        out_specs=pl.BlockSpec((tm, tn), lambda i, j: (i, j)),
        out_shape=jax.ShapeDtypeStruct((m, d), F32),
        compiler_params=_params(("arbitrary", "arbitrary")),
        name="out_proj",
    )(ret, dif, mlp, w_out, w_out, w_out, xx, mod)


def _final_norm_kernel(x_ref, w_ref, o_ref):
    x = x_ref[...]
    o_ref[...] = x * lax.rsqrt(jnp.mean(x * x, axis=-1, keepdims=True) + EPS) * w_ref[...]


def _final_norm(xx, w, seq):
    d = xx.shape[1]
    tr = 256
    return pl.pallas_call(
        _final_norm_kernel,
        grid=(seq // tr,),
        in_specs=[pl.BlockSpec((tr, d), lambda i: (i, 0)), pl.BlockSpec((1, d), lambda i: (0, 0))],
        out_specs=pl.BlockSpec((tr, d), lambda i: (i, 0)),
        out_shape=jax.ShapeDtypeStruct((seq, d), F32),
        compiler_params=_params(("arbitrary",)),
        name="final_norm",
    )(xx, w)


def _rope_tables(seq, ctx, dim, repeats):
    rows = seq // GRID_W
    n_freq = dim // 4
    inv = ROPE_BASE ** (-jnp.arange(n_freq, dtype=F32) / n_freq)
    ang_r = jnp.arange(rows, dtype=F32)[:, None] * inv
    ang_c = jnp.arange(GRID_W, dtype=F32)[:, None] * inv

    def per_token(f):
        by_row = jnp.broadcast_to(f(ang_r)[:, None, :], (rows, GRID_W, n_freq))
        by_col = jnp.broadcast_to(f(ang_c)[None, :, :], (rows, GRID_W, n_freq))
        return jnp.concatenate([by_row, by_col], axis=-1).reshape(seq, 2 * n_freq)

    cos_t, sin_t = per_token(jnp.cos), per_token(jnp.sin)
    cos = jnp.concatenate([cos_t, cos_t], axis=-1)
    sin = jnp.concatenate([-sin_t, sin_t], axis=-1)
    cos = jnp.concatenate([jnp.tile(cos, (1, repeats)), jnp.ones((ctx, repeats * dim), F32)], axis=0)
    sin = jnp.concatenate([jnp.tile(sin, (1, repeats)), jnp.zeros((ctx, repeats * dim), F32)], axis=0)
    return cos, sin


def kernel(x, c, ctx, c_ctx, w_in, w_out, w_ada, b_ada, norm_w, ret_decay_fwd, ret_decay_bwd,
           lam_q1, lam_k1, lam_q2, lam_k2, dif_norm_w, mlp_vnorm_w, mlp_ws, mlp_bs, final_norm_w):
    batch, seq, d = x.shape
    assert batch == 1, "the kernels process one sequence"
    n_ctx = ctx.shape[1]
    depth = w_in.shape[0]
    heads = ret_decay_fwd.shape[1]
    ret_w = heads * HEAD_DIM
    dif_w = ret_w
    mlp_w = mlp_vnorm_w.shape[1]
    assert 2 * ret_w + mlp_w == d and w_in.shape[2] == 8 * ret_w + 3 * mlp_w

    rk_blk, rv_blk, dk_blk, dv_blk = 0, heads, 2 * heads, 3 * heads
    rq_blk, dq_blk, rg_blk, dg_blk = 4 * heads, 5 * heads, 6 * heads, 7 * heads
    mu_blk = 8 * ret_w // mlp_w

    xx = (x[0], ctx[0])
    cc = jnp.zeros((8, d), F32).at[0].set(c[0]).at[1].set(c_ctx)
    mods = _modulation(cc, w_ada, b_ada)
    cos_r, sin_r = _rope_tables(seq, n_ctx, HEAD_DIM, 1)
    cos_d, sin_d = _rope_tables(seq, n_ctx, HEAD_DIM // 2, 2)

    for l in range(depth):
        lambda_init = 0.8 - 0.6 * math.exp(-0.3 * l)
        h = _norm_mod(xx, norm_w[l][None, :], mods[l], seq)
        if l == 0:
            h, xx = h
        z = _in_proj(h, w_in, l, BF16)
        ret = _retention(z, ret_decay_fwd, ret_decay_bwd, cos_r, sin_r, layer=l, heads=heads, seq=seq,
                         q_blk=rq_blk, k_blk=rk_blk, v_blk=rv_blk, g_blk=rg_blk)
        lams = [a[l][None, :] for a in (lam_q1, lam_k1, lam_q2, lam_k2)]
        dif = _attention(z, lams, cos_d, sin_d, dif_norm_w[l][None, :], lambda_init=lambda_init,
                         heads=heads, seq=seq, q_blk=dq_blk, k_blk=dk_blk, v_blk=dv_blk, g_blk=dg_blk)
        mlp = _chunk_mlp(z, mlp_vnorm_w[l][None, :], mlp_ws[l], mlp_bs[l].T, u_blk=mu_blk, width=mlp_w)
        xx = _out_proj(ret, dif, mlp, w_out, l, xx, mods[l], seq=seq)

    return _final_norm(xx, final_norm_w[None, :], seq)[None]
```

```python
import functools
import math

import jax
import jax.numpy as jnp
from jax import lax
from jax.experimental import pallas as pl
from jax.experimental.pallas import tpu as pltpu

HEAD_DIM = 128
CHUNK = 128
GRID_W = 64
ROPE_BASE = 10000.0
EPS = 1e-6
LOG2E = 1.4426950408889634
NEG_BIG = -1e30

LANES = 128
BF16_SUBLANES = 16
VMEM_BYTES = 64 * 1024 * 1024
VMEM_LIMIT = VMEM_BYTES - 8 * 1024 * 1024

ROW_TILE = 256
MOD_TN = (512, 256, 128)
IN_PROJ_TM = (1408, 1056, 768, 640, 256)
IN_PROJ_TN = (512, 256)
OUT_PROJ_TM = (1056, 768, 640, 256)
OUT_PROJ_TN = (512, 256)
ATTN_TQ = 256
ATTN_CHUNKS = 4
N_PARTS = 4
RET_UNROLL = 8

BF16 = jnp.bfloat16
F32 = jnp.float32

NT_DIMS = (((1,), (1,)), ((), ()))
TN_DIMS = (((0,), (0,)), ((), ()))


def _params(semantics):
    return pltpu.CompilerParams(dimension_semantics=semantics, vmem_limit_bytes=VMEM_LIMIT)


def _pick(n, candidates):
    for c in candidates:
        if n % c == 0:
            return c
    return n


def _silu(x):
    return x * (1.0 / (1.0 + jnp.exp(-x)))


def _gelu(x):
    return 0.5 * x * (1.0 + lax.erf(x * (2.0 ** -0.5)))


def _mod_kernel(cc_ref, w_ref, b_ref, o_ref):
    a = _silu(cc_ref[...]).astype(BF16)
    o_ref[...] = jnp.dot(a, w_ref[...].astype(BF16), preferred_element_type=F32) + b_ref[...]


def _modulation(cc, w_ada, b_ada):
    depth, d, n3 = w_ada.shape
    tn = _pick(n3, MOD_TN)
    return pl.pallas_call(
        _mod_kernel,
        grid=(depth, n3 // tn),
        in_specs=[
            pl.BlockSpec((8, d), lambda l, j: (0, 0)),
            pl.BlockSpec((None, d, tn), lambda l, j: (l, 0, j)),
            pl.BlockSpec((None, 1, tn), lambda l, j: (l, 0, j)),
        ],
        out_specs=pl.BlockSpec((None, 8, tn), lambda l, j: (l, 0, j)),
        out_shape=jax.ShapeDtypeStruct((depth, 8, n3), F32),
        compiler_params=_params(("arbitrary", "arbitrary")),
        name="modulation",
    )(cc, w_ada, b_ada.reshape(depth, 1, n3))


def _norm_mod_kernel(*refs, d, n_latent_blocks, joins_streams):
    is_ctx = pl.program_id(0) >= n_latent_blocks
    if joins_streams:
        x_ref, c_ref, nw_ref, mod_ref, h_ref, xx_ref = refs
        x = jnp.where(is_ctx, c_ref[...], x_ref[...])
        xx_ref[...] = x
    else:
        x_ref, nw_ref, mod_ref, h_ref = refs
        x = x_ref[...]
    y = x * lax.rsqrt(jnp.mean(x * x, axis=-1, keepdims=True) + EPS) * nw_ref[...]
    shift = jnp.where(is_ctx, mod_ref[1:2, 0:d], mod_ref[0:1, 0:d])
    scale = jnp.where(is_ctx, mod_ref[1:2, d:2 * d], mod_ref[0:1, d:2 * d])
    h_ref[...] = (y * (1.0 + scale) + shift).astype(h_ref.dtype)


def _norm_mod(streams, nw, mod, seq):
    joins_streams = isinstance(streams, tuple)
    tr = ROW_TILE
    if joins_streams:
        x, c = streams
        d = x.shape[1]
        m = seq + c.shape[0]
        n_latent_blocks = seq // tr
        operands = (x, c)
        stream_specs = [pl.BlockSpec((tr, d), lambda i: (jnp.minimum(i, n_latent_blocks - 1), 0)),
                        pl.BlockSpec((tr, d), lambda i: (jnp.maximum(i - n_latent_blocks, 0), 0))]
    else:
        m, d = streams.shape
        operands = (streams,)
        stream_specs = [pl.BlockSpec((tr, d), lambda i: (i, 0))]
    row_block = pl.BlockSpec((tr, d), lambda i: (i, 0))
    out = pl.pallas_call(
        functools.partial(_norm_mod_kernel, d=d, n_latent_blocks=seq // tr, joins_streams=joins_streams),
        grid=(m // tr,),
        in_specs=stream_specs + [pl.BlockSpec((1, d), lambda i: (0, 0)),
                                 pl.BlockSpec((8, 3 * d), lambda i: (0, 0))],
        out_specs=(row_block, row_block) if joins_streams else row_block,
        out_shape=((jax.ShapeDtypeStruct((m, d), BF16), jax.ShapeDtypeStruct((m, d), F32))
                   if joins_streams else jax.ShapeDtypeStruct((m, d), BF16)),
        compiler_params=_params(("arbitrary",)),
        name="norm_mod",
    )(*operands, nw, mod)
    return out


def _matmul_kernel(a_ref, b_ref, o_ref):
    o_ref[...] = jnp.dot(a_ref[...], b_ref[...].astype(BF16), preferred_element_type=F32).astype(o_ref.dtype)


def _in_proj(h, w, layer, out_dtype):
    m, k = h.shape
    n = w.shape[2]
    tm = _pick(m, IN_PROJ_TM)
    tn = _pick(n, IN_PROJ_TN)
    return pl.pallas_call(
        _matmul_kernel,
        grid=(m // tm, n // tn),
        in_specs=[
            pl.BlockSpec((tm, k), lambda i, j: (i, 0)),
            pl.BlockSpec((None, k, tn), lambda i, j: (layer, 0, j)),
        ],
        out_specs=pl.BlockSpec((tm, tn), lambda i, j: (i, j)),
        out_shape=jax.ShapeDtypeStruct((m, n), out_dtype),
        compiler_params=_params(("arbitrary", "arbitrary")),
        name="in_proj",
    )(h, w)


def _retention_kernel(decf_ref, decb_ref, q_ref, k_ref, v_ref, g_ref, cos_ref, sin_ref, o_ref,
                      sf_ref, sb_ref, kvb_ref, lhs_ref, *, layer, n_latent_chunks, n_ctx_chunks):
    head = pl.program_id(0)
    lgf = jnp.log1p(-jnp.exp(jnp.full((1, LANES), decf_ref[layer, head], F32)))
    lgb = jnp.log1p(-jnp.exp(jnp.full((1, LANES), decb_ref[layer, head], F32)))
    ri = lax.broadcasted_iota(jnp.int32, (CHUNK, CHUNK), 0)
    ci = lax.broadcasted_iota(jnp.int32, (CHUNK, CHUNK), 1)
    rel = (ri - ci).astype(F32)
    pos = ri.astype(F32)
    dmat = jnp.where(rel >= 0, jnp.exp(lgf * jnp.maximum(rel, 0.0)), jnp.exp(lgb * jnp.maximum(-rel, 0.0)))
    wfk = jnp.exp(lgf * (CHUNK - 1.0 - pos))
    wbk = jnp.exp(lgb * pos)
    dqf = jnp.exp(lgf * (pos + 1.0))
    dqb = jnp.exp(lgb * (CHUNK - pos))
    cdf = jnp.exp(lgf * float(CHUNK))
    cdb = jnp.exp(lgb * float(CHUNK))
    ret_scale = HEAD_DIM ** -0.5

    def rows_of(c):
        return pl.ds(pl.multiple_of(c * CHUNK, CHUNK), CHUNK)

    def rope(ref, rows):
        x = ref[rows, :].astype(F32)
        return x * cos_ref[rows, :] + pltpu.roll(x, HEAD_DIM // 2, 1) * sin_ref[rows, :]

    def scan_states(first, n, sf, sb):
        def up(t, sf):
            c = first + t
            rows = rows_of(c)
            qr = rope(q_ref, rows)
            kr = rope(k_ref, rows) * ret_scale
            a = lax.dot_general(qr.astype(BF16), kr.astype(BF16), NT_DIMS, preferred_element_type=F32) * dmat
            lhs_ref[c] = jnp.concatenate(
                [a.astype(BF16), (qr * dqf).astype(BF16), (qr * dqb).astype(BF16)], axis=1)
            kk = jnp.concatenate([(kr * wfk).astype(BF16), (kr * wbk).astype(BF16)], axis=1)
            kv = lax.dot_general(kk, v_ref[rows, :].astype(BF16), TN_DIMS, preferred_element_type=F32)
            sf_ref[c] = sf.astype(BF16)
            kvb_ref[c] = kv[HEAD_DIM:, :]
            return sf * cdf + kv[:HEAD_DIM, :]

        sf = lax.fori_loop(0, n, up, sf, unroll=min(n, RET_UNROLL))

        def down(t, sb):
            c = first + n - 1 - t
            sb_ref[c] = sb.astype(BF16)
            return sb * cdb + kvb_ref[c]

        sb = lax.fori_loop(0, n, down, sb, unroll=min(n, RET_UNROLL))
        return sf, sb

    def outputs(first, n):
        def one(t, carry):
            c = first + t
            rows = rows_of(c)
            rhs = jnp.concatenate([v_ref[rows, :].astype(BF16), sf_ref[c], sb_ref[c]], axis=0)
            o = jnp.dot(lhs_ref[c], rhs, preferred_element_type=F32)
            o = o * lax.rsqrt(jnp.mean(o * o, axis=-1, keepdims=True) + EPS)
            o_ref[rows, :] = (o * _silu(g_ref[rows, :].astype(F32))).astype(o_ref.dtype)
            return carry

        lax.fori_loop(0, n, one, 0, unroll=min(n, RET_UNROLL))

    zero = jnp.zeros((HEAD_DIM, HEAD_DIM), F32)
    sf, sb = scan_states(n_latent_chunks, n_ctx_chunks, zero, zero)
    scan_states(0, n_latent_chunks, sf, sb)
    outputs(0, n_latent_chunks)
    outputs(n_latent_chunks, n_ctx_chunks)


def _retention(z, dec_f, dec_b, cos, sin, *, layer, heads, seq, q_blk, k_blk, v_blk, g_blk):
    m = z.shape[0]
    col = lambda base: pl.BlockSpec((m, HEAD_DIM), lambda h: (0, base + h))
    smem = pl.BlockSpec(memory_space=pltpu.SMEM)
    full = pl.BlockSpec((m, HEAD_DIM), lambda h: (0, 0), pipeline_mode=pl.Buffered(1))
    return pl.pallas_call(
        functools.partial(_retention_kernel, layer=layer, n_latent_chunks=seq // CHUNK,
                          n_ctx_chunks=(m - seq) // CHUNK),
        grid=(heads,),
        in_specs=[smem, smem, col(q_blk), col(k_blk), col(v_blk), col(g_blk), full, full],
        out_specs=pl.BlockSpec((m, HEAD_DIM), lambda h: (0, h)),
        out_shape=jax.ShapeDtypeStruct((m, heads * HEAD_DIM), BF16),
        scratch_shapes=[pltpu.VMEM((m // CHUNK, HEAD_DIM, HEAD_DIM), BF16),
                        pltpu.VMEM((m // CHUNK, HEAD_DIM, HEAD_DIM), BF16),
                        pltpu.VMEM((m // CHUNK, HEAD_DIM, HEAD_DIM), F32),
                        pltpu.VMEM((m // CHUNK, CHUNK, 3 * HEAD_DIM), BF16)],
        compiler_params=_params(("arbitrary",)),
        name="retention",
    )(dec_f, dec_b, z, z, z, z, cos, sin)


def _swap_half_pairs(x):
    lane = lax.broadcasted_iota(jnp.int32, x.shape, 1)
    return jnp.where(lane % 64 < 32, pltpu.roll(x, 96, 1), pltpu.roll(x, 32, 1))


def _dif_rope(x, cos, sin):
    return x * cos + _swap_half_pairs(x) * sin


def _attention_kernel(lq1_ref, lk1_ref, lq2_ref, lk2_ref, q_ref, cosq_ref, sinq_ref, qn_ref, cosn_ref, sinn_ref,
                      k_ref, v_ref, cosk_ref, sink_ref, gp_ref, nw_ref, o_ref,
                      ks_ref, vtx_ref, vtl_ref, s0_ref, s1_ref, cm_ref, acc_ref, *, lambda_init, seq, tq, tk):
    qi = pl.program_id(1)
    m = k_ref.shape[0]
    ctx = m - seq
    vrows = vtl_ref.shape[0]
    n_chunks = seq // tk
    last = n_chunks - 1
    ones_rows = (lax.broadcasted_iota(jnp.int32, (vrows - HEAD_DIM, 1), 0) == 0).astype(BF16)
    slots = (s0_ref, s1_ref)

    def chunk_len(c):
        return tk + ctx if c == last else tk

    @pl.when(qi == 0)
    def _():
        def prep_rows(rows, n):
            k = _dif_rope(k_ref[rows, :].astype(F32), cosk_ref[rows, :], sink_ref[rows, :])
            ks_ref[rows, :] = k.astype(BF16)
            vt = v_ref[rows, :].astype(F32).T.astype(BF16)
            return jnp.concatenate([vt, jnp.broadcast_to(ones_rows, (vrows - HEAD_DIM, n))], axis=0)

        def prep(c, carry):
            vtx_ref[c] = prep_rows(pl.ds(pl.multiple_of(c * tk, tk), tk), tk)
            return carry
        lax.fori_loop(0, last, prep, 0)
        vtl_ref[:, 0:tk] = prep_rows(pl.ds(last * tk, tk), tk)
        vtl_ref[:, tk:tk + ctx] = prep_rows(pl.ds(seq, ctx), ctx)

    def query_operands(ref, cos_ref, sin_ref):
        q = _dif_rope(ref[...].astype(F32), cos_ref[...], sin_ref[...]) * ((HEAD_DIM // 2) ** -0.5 * LOG2E)
        lane = lax.broadcasted_iota(jnp.int32, q.shape, 1)
        q0 = jnp.where(lane < HEAD_DIM // 2, q, 0.0).astype(BF16)
        q1 = jnp.where(lane >= HEAD_DIM // 2, q, 0.0).astype(BF16)
        return q0, q1

    def scores(qs, row0, n, slot, off):
        kb = ks_ref[pl.ds(row0, n), :]
        cmax = []
        for comp, qm in enumerate(qs):
            s = lax.dot_general(kb, qm, NT_DIMS, preferred_element_type=F32)
            slots[slot][comp, off:off + n, :] = s
            cmax.append(jnp.max(s, axis=0, keepdims=True))
        return tuple(cmax)

    def rescale(cmax, state):
        new = []
        for comp in range(2):
            mx, acc = state[comp]
            mn = jnp.maximum(mx, cmax[comp])
            new.append((mn, jnp.exp2(mx - mn) * acc))
        return tuple(new)

    def accumulate(slot, off, n, vt, state):
        new = []
        for comp in range(2):
            mn, acc = state[comp]
            p = jnp.exp2(slots[slot][comp, off:off + n, :] - mn).astype(BF16)
            new.append((mn, acc + jnp.dot(vt, p, preferred_element_type=F32)))
        return tuple(new)

    def parts(c):
        step = tk // N_PARTS
        out = [(j * step, step) for j in range(N_PARTS)]
        if c == last:
            out[-1] = (out[-1][0], step + ctx)
        return out

    def vt_part(c, off, n):
        return vtl_ref[:, off:off + n] if c == last else vtx_ref[c, :, off:off + n]

    def finalize_previous():
        acc0 = acc_ref[0]
        acc1 = acc_ref[1]
        lam = (jnp.exp(jnp.sum(lq1_ref[...] * lk1_ref[...], axis=-1, keepdims=True))
               - jnp.exp(jnp.sum(lq2_ref[...] * lk2_ref[...], axis=-1, keepdims=True)) + lambda_init)
        l0 = acc0[HEAD_DIM:HEAD_DIM + 1, :]
        l1 = acc1[HEAD_DIM:HEAD_DIM + 1, :]
        ot = acc0[:HEAD_DIM, :] * (1.0 / l0) - lam * (acc1[:HEAD_DIM, :] * (1.0 / l1))
        o = ot.T
        o = o * lax.rsqrt(jnp.mean(o * o, axis=-1, keepdims=True) + EPS) * nw_ref[...] * (1.0 - lambda_init)
        o_ref[...] = (o * _silu(gp_ref[...].astype(F32))).astype(o_ref.dtype)

    def leave_sums(state):
        (_, acc0), (_, acc1) = state
        acc_ref[0] = acc0
        acc_ref[1] = acc1

    init = ((jnp.full((1, tq), NEG_BIG, F32), jnp.zeros((vrows, tq), F32)),) * 2
    qs = query_operands(q_ref, cosq_ref, sinq_ref)
    n_latent_blocks = seq // tq

    @pl.when(qi == 0)
    def _():
        cmax = scores(qs, 0, chunk_len(0), 0, 0)
        cm_ref[0] = cmax[0]
        cm_ref[1] = cmax[1]
        acc_ref[...] = jnp.ones(acc_ref.shape, F32)

    @pl.when(qi < n_latent_blocks)
    def _():
        finalize_previous()
        cmax = (cm_ref[0], cm_ref[1])
        state = init
        for c in range(n_chunks):
            state = rescale(cmax, state)
            nc = (c + 1) % n_chunks
            nqs = qs if c < last else query_operands(qn_ref, cosn_ref, sinn_ref)
            nxt = None
            for (off, n), (noff, nn) in zip(parts(c), parts(nc)):
                part = scores(nqs, nc * tk + noff, nn, nc % 2, noff)
                nxt = part if nxt is None else tuple(jnp.maximum(a, b) for a, b in zip(nxt, part))
                state = accumulate(c % 2, off, n, vt_part(c, off, n), state)
            cmax = nxt
        cm_ref[0] = cmax[0]
        cm_ref[1] = cmax[1]
        leave_sums(state)

    @pl.when(qi == n_latent_blocks)
    def _():
        finalize_previous()
        cmax = scores(qs, seq, ctx, 1, 0)
        leave_sums(accumulate(1, 0, ctx, vtl_ref[:, tk:tk + ctx], rescale(cmax, init)))

    @pl.when(qi > n_latent_blocks)
    def _():
        finalize_previous()


def _attention(z, lams, cos, sin, nw, *, lambda_init, heads, seq, q_blk, k_blk, v_blk, g_blk):
    m = z.shape[0]
    ctx = m - seq
    tq = ATTN_TQ
    n_chunks = ATTN_CHUNKS
    tk = seq // n_chunks
    assert seq % tq == 0 and ctx == tq, "one query block must hold exactly the context rows"
    assert seq % (n_chunks * LANES) == 0 and n_chunks % 2 == 0 and n_chunks >= 2 and tk % N_PARTS == 0
    nq = m // tq
    vrows = HEAD_DIM + BF16_SUBLANES
    lam_spec = pl.BlockSpec((1, HEAD_DIM // 2), lambda h, i: (0, 0))
    this_row = lambda base, per_head: pl.BlockSpec(
        (tq, HEAD_DIM), lambda h, i: (jnp.minimum(i, nq - 1), base + h * per_head))
    next_row = lambda base, per_head: pl.BlockSpec(
        (tq, HEAD_DIM), lambda h, i: (jnp.minimum(i + 1, nq - 1), base + h * per_head))
    prev_row = lambda base: pl.BlockSpec((tq, HEAD_DIM), lambda h, i: (jnp.maximum(i - 1, 0), base + h))
    kcol = lambda base: pl.BlockSpec((m, HEAD_DIM), lambda h, i: (0, base + h))
    table = pl.BlockSpec((m, HEAD_DIM), lambda h, i: (0, 0), pipeline_mode=pl.Buffered(1))
    return pl.pallas_call(
        functools.partial(_attention_kernel, lambda_init=lambda_init, seq=seq, tq=tq, tk=tk),
        grid=(heads, nq + 1),
        in_specs=[lam_spec, lam_spec, lam_spec, lam_spec,
                  this_row(q_blk, 1), this_row(0, 0), this_row(0, 0),
                  next_row(q_blk, 1), next_row(0, 0), next_row(0, 0),
                  kcol(k_blk), kcol(v_blk), table, table,
                  prev_row(g_blk),
                  pl.BlockSpec((1, HEAD_DIM), lambda h, i: (0, 0))],
        out_specs=prev_row(0),
        out_shape=jax.ShapeDtypeStruct((m, heads * HEAD_DIM), BF16),
        scratch_shapes=[pltpu.VMEM((m, HEAD_DIM), BF16),
                        pltpu.VMEM((n_chunks - 1, vrows, tk), BF16),
                        pltpu.VMEM((vrows, tk + ctx), BF16),
                        pltpu.VMEM((2, tk + ctx, tq), F32),
                        pltpu.VMEM((2, tk + ctx, tq), F32),
                        pltpu.VMEM((2, 1, tq), F32),
                        pltpu.VMEM((2, vrows, tq), F32)],
        compiler_params=_params(("arbitrary", "arbitrary")),
        name="diff_attention",
    )(*lams, z, cos, sin, z, cos, sin, z, z, cos, sin, z, nw)


def _mlp_kernel(u_ref, v_ref, g_ref, vnw_ref, ws_ref, bst_ref, o_ref, *, groups):
    tr, width = u_ref.shape
    gd = width // groups
    v = _gelu(v_ref[...].astype(F32))
    v = (v * lax.rsqrt(jnp.mean(v * v, axis=-1, keepdims=True) + EPS) * vnw_ref[...]).astype(BF16)
    for c in range(tr // CHUNK):
        rows = slice(c * CHUNK, (c + 1) * CHUNK)
        for g in range(groups):
            cols = slice(g * gd, (g + 1) * gd)
            mixed = jnp.dot(ws_ref[g].astype(BF16), v[rows, cols], preferred_element_type=F32) + bst_ref[:, g:g + 1]
            u = _gelu(u_ref[rows, cols].astype(F32))
            o_ref[rows, cols] = (u * mixed * _silu(g_ref[rows, cols].astype(F32))).astype(o_ref.dtype)


def _chunk_mlp(z, vnw, ws, bst, *, u_blk, width):
    m = z.shape[0]
    groups = ws.shape[0]
    tr = ROW_TILE
    col = lambda b: pl.BlockSpec((tr, width), lambda i: (i, b))
    return pl.pallas_call(
        functools.partial(_mlp_kernel, groups=groups),
        grid=(m // tr,),
        in_specs=[col(u_blk), col(u_blk + 1), col(u_blk + 2),
                  pl.BlockSpec((1, width), lambda i: (0, 0)),
                  pl.BlockSpec((groups, CHUNK, CHUNK), lambda i: (0, 0, 0)),
                  pl.BlockSpec((CHUNK, groups), lambda i: (0, 0))],
        out_specs=pl.BlockSpec((tr, width), lambda i: (i, 0)),
        out_shape=jax.ShapeDtypeStruct((m, width), BF16),
        compiler_params=_params(("arbitrary",)),
        name="chunk_mlp",
    )(z, z, z, vnw, ws, bst)


def _out_proj_kernel(ret_ref, dif_ref, mlp_ref, wr_ref, wd_ref, wm_ref, x_ref, gate_ref, o_ref, *, seq, tm):
    y = jnp.dot(ret_ref[...], wr_ref[...].astype(BF16), preferred_element_type=F32)
    y = y + jnp.dot(dif_ref[...], wd_ref[...].astype(BF16), preferred_element_type=F32)
    y = y + jnp.dot(mlp_ref[...], wm_ref[...].astype(BF16), preferred_element_type=F32)
    row = pl.program_id(0) * tm + lax.broadcasted_iota(jnp.int32, y.shape, 0)
    gate = jnp.where(row >= seq, gate_ref[1:2, :], gate_ref[0:1, :])
    o_ref[...] = x_ref[...] + gate * y


def _out_proj(ret, dif, mlp, w_out, layer, xx, mod, *, seq):
    m, d = xx.shape
    rw, dw, mw = ret.shape[1], dif.shape[1], mlp.shape[1]
    assert rw == dw and (rw + dw) % mw == 0
    tm = _pick(m, OUT_PROJ_TM)
    tn = _pick(d, OUT_PROJ_TN)
    gate_blk = 2 * d // tn
    return pl.pallas_call(
        functools.partial(_out_proj_kernel, seq=seq, tm=tm),
        grid=(m // tm, d // tn),
        in_specs=[
            pl.BlockSpec((tm, rw), lambda i, j: (i, 0)),
            pl.BlockSpec((tm, dw), lambda i, j: (i, 0)),
            pl.BlockSpec((tm, mw), lambda i, j: (i, 0)),
            pl.BlockSpec((None, rw, tn), lambda i, j: (layer, 0, j)),
            pl.BlockSpec((None, dw, tn), lambda i, j: (layer, 1, j)),
            pl.BlockSpec((None, mw, tn), lambda i, j: (layer, (rw + dw) // mw, j)),
            pl.BlockSpec((tm, tn), lambda i, j: (i, j)),
            pl.BlockSpec((8, tn), lambda i, j: (0, gate_blk + j)),
        ],
        out_specs=pl.BlockSpec((tm, tn), lambda i, j: (i, j)),
        out_shape=jax.ShapeDtypeStruct((m, d), F32),
        compiler_params=_params(("arbitrary", "arbitrary")),
        name="out_proj",
    )(ret, dif, mlp, w_out, w_out, w_out, xx, mod)


def _final_norm_kernel(x_ref, w_ref, o_ref):
    x = x_ref[...]
    o_ref[...] = x * lax.rsqrt(jnp.mean(x * x, axis=-1, keepdims=True) + EPS) * w_ref[...]


def _final_norm(xx, w, seq):
    d = xx.shape[1]
    tr = ROW_TILE
    return pl.pallas_call(
        _final_norm_kernel,
        grid=(seq // tr,),
        in_specs=[pl.BlockSpec((tr, d), lambda i: (i, 0)), pl.BlockSpec((1, d), lambda i: (0, 0))],
        out_specs=pl.BlockSpec((tr, d), lambda i: (i, 0)),
        out_shape=jax.ShapeDtypeStruct((seq, d), F32),
        compiler_params=_params(("arbitrary",)),
        name="final_norm",
    )(xx, w)


def _rope_tables(seq, ctx, dim, repeats):
    rows = seq // GRID_W
    n_freq = dim // 4
    inv = ROPE_BASE ** (-jnp.arange(n_freq, dtype=F32) / n_freq)
    ang_r = jnp.arange(rows, dtype=F32)[:, None] * inv
    ang_c = jnp.arange(GRID_W, dtype=F32)[:, None] * inv

    def per_token(f):
        by_row = jnp.broadcast_to(f(ang_r)[:, None, :], (rows, GRID_W, n_freq))
        by_col = jnp.broadcast_to(f(ang_c)[None, :, :], (rows, GRID_W, n_freq))
        return jnp.concatenate([by_row, by_col], axis=-1).reshape(seq, 2 * n_freq)

    cos_t, sin_t = per_token(jnp.cos), per_token(jnp.sin)
    cos = jnp.concatenate([cos_t, cos_t], axis=-1)
    sin = jnp.concatenate([-sin_t, sin_t], axis=-1)
    cos = jnp.concatenate([jnp.tile(cos, (1, repeats)), jnp.ones((ctx, repeats * dim), F32)], axis=0)
    sin = jnp.concatenate([jnp.tile(sin, (1, repeats)), jnp.zeros((ctx, repeats * dim), F32)], axis=0)
    return cos, sin


def kernel(x, c, ctx, c_ctx, w_in, w_out, w_ada, b_ada, norm_w, ret_decay_fwd, ret_decay_bwd,
           lam_q1, lam_k1, lam_q2, lam_k2, dif_norm_w, mlp_vnorm_w, mlp_ws, mlp_bs, final_norm_w):
    batch, seq, d = x.shape
    assert batch == 1, "the kernels process one sequence"
    n_ctx = ctx.shape[1]
    depth = w_in.shape[0]
    heads = ret_decay_fwd.shape[1]
    ret_w = heads * HEAD_DIM
    dif_w = ret_w
    mlp_w = mlp_vnorm_w.shape[1]
    assert 2 * ret_w + mlp_w == d and w_in.shape[2] == 8 * ret_w + 3 * mlp_w

    rk_blk, rv_blk, dk_blk, dv_blk = 0, heads, 2 * heads, 3 * heads
    rq_blk, dq_blk, rg_blk, dg_blk = 4 * heads, 5 * heads, 6 * heads, 7 * heads
    mu_blk = 8 * ret_w // mlp_w

    xx = (x[0], ctx[0])
    cc = jnp.zeros((8, d), F32).at[0].set(c[0]).at[1].set(c_ctx)
    mods = _modulation(cc, w_ada, b_ada)
    cos_r, sin_r = _rope_tables(seq, n_ctx, HEAD_DIM, 1)
    cos_d, sin_d = _rope_tables(seq, n_ctx, HEAD_DIM // 2, 2)

    for l in range(depth):
        lambda_init = 0.8 - 0.6 * math.exp(-0.3 * l)
        h = _norm_mod(xx, norm_w[l][None, :], mods[l], seq)
        if l == 0:
            h, xx = h
        z = _in_proj(h, w_in, l, BF16)
        ret = _retention(z, ret_decay_fwd, ret_decay_bwd, cos_r, sin_r, layer=l, heads=heads, seq=seq,
                         q_blk=rq_blk, k_blk=rk_blk, v_blk=rv_blk, g_blk=rg_blk)
        lams = [a[l][None, :] for a in (lam_q1, lam_k1, lam_q2, lam_k2)]
        dif = _attention(z, lams, cos_d, sin_d, dif_norm_w[l][None, :], lambda_init=lambda_init,
                         heads=heads, seq=seq, q_blk=dq_blk, k_blk=dk_blk, v_blk=dv_blk, g_blk=dg_blk)
        mlp = _chunk_mlp(z, mlp_vnorm_w[l][None, :], mlp_ws[l], mlp_bs[l].T, u_blk=mu_blk, width=mlp_w)
        xx = _out_proj(ret, dif, mlp, w_out, l, xx, mods[l], seq=seq)

    return _final_norm(xx, final_norm_w[None, :], seq)[None]
```

```python
import functools
import math

import jax
import jax.numpy as jnp
from jax import lax
from jax.experimental import pallas as pl
from jax.experimental.pallas import tpu as pltpu

HEAD_DIM = 128
CHUNK = 128
GRID_W = 64
ROPE_BASE = 10000.0
EPS = 1e-6
LOG2E = 1.4426950408889634
NEG_BIG = -1e30

LANES = 128
BF16_SUBLANES = 16
VMEM_BYTES = 64 * 1024 * 1024
VMEM_LIMIT = VMEM_BYTES - 8 * 1024 * 1024

ROW_TILE = 256
WIDE_ROW_TILES = (768, 512, 256)
MOD_TN = (512, 256, 128)
IN_PROJ_TM = (1408, 1056, 768, 640, 256)
IN_PROJ_TN = (512, 256)
OUT_PROJ_TM = (1056, 768, 640, 256)
OUT_PROJ_TN = (512, 256)
ATTN_TQ = 256
ATTN_CHUNKS = 4
N_PARTS = 4
RET_UNROLL = 8

BF16 = jnp.bfloat16
F32 = jnp.float32

NT_DIMS = (((1,), (1,)), ((), ()))
TN_DIMS = (((0,), (0,)), ((), ()))


def _params(semantics):
    return pltpu.CompilerParams(dimension_semantics=semantics, vmem_limit_bytes=VMEM_LIMIT)


def _pick(n, candidates):
    for c in candidates:
        if n % c == 0:
            return c
    return n


def _silu(x):
    return x * (1.0 / (1.0 + jnp.exp(-x)))


def _gelu(x):
    return 0.5 * x * (1.0 + lax.erf(x * (2.0 ** -0.5)))


def _mod_kernel(cc_ref, w_ref, b_ref, o_ref):
    a = _silu(cc_ref[...]).astype(BF16)
    o_ref[...] = jnp.dot(a, w_ref[...].astype(BF16), preferred_element_type=F32) + b_ref[...]


def _modulation(cc, w_ada, b_ada):
    depth, d, n3 = w_ada.shape
    tn = _pick(n3, MOD_TN)
    return pl.pallas_call(
        _mod_kernel,
        grid=(depth, n3 // tn),
        in_specs=[
            pl.BlockSpec((8, d), lambda l, j: (0, 0)),
            pl.BlockSpec((None, d, tn), lambda l, j: (l, 0, j)),
            pl.BlockSpec((None, 1, tn), lambda l, j: (l, 0, j)),
        ],
        out_specs=pl.BlockSpec((None, 8, tn), lambda l, j: (l, 0, j)),
        out_shape=jax.ShapeDtypeStruct((depth, 8, n3), F32),
        compiler_params=_params(("arbitrary", "arbitrary")),
        name="modulation",
    )(cc, w_ada, b_ada.reshape(depth, 1, n3))


def _norm_mod_kernel(*refs, d, n_latent_blocks, joins_streams):
    is_ctx = pl.program_id(0) >= n_latent_blocks
    if joins_streams:
        x_ref, c_ref, nw_ref, mod_ref, h_ref, xx_ref = refs
        x = jnp.where(is_ctx, c_ref[...], x_ref[...])
        xx_ref[...] = x
    else:
        x_ref, nw_ref, mod_ref, h_ref = refs
        x = x_ref[...]
    y = x * lax.rsqrt(jnp.mean(x * x, axis=-1, keepdims=True) + EPS) * nw_ref[...]
    shift = jnp.where(is_ctx, mod_ref[1:2, 0:d], mod_ref[0:1, 0:d])
    scale = jnp.where(is_ctx, mod_ref[1:2, d:2 * d], mod_ref[0:1, d:2 * d])
    h_ref[...] = (y * (1.0 + scale) + shift).astype(h_ref.dtype)


def _norm_mod(streams, nw, mod, seq):
    joins_streams = isinstance(streams, tuple)
    tr = ROW_TILE
    if joins_streams:
        x, c = streams
        d = x.shape[1]
        m = seq + c.shape[0]
        n_latent_blocks = seq // tr
        operands = (x, c)
        stream_specs = [pl.BlockSpec((tr, d), lambda i: (jnp.minimum(i, n_latent_blocks - 1), 0)),
                        pl.BlockSpec((tr, d), lambda i: (jnp.maximum(i - n_latent_blocks, 0), 0))]
    else:
        m, d = streams.shape
        operands = (streams,)
        stream_specs = [pl.BlockSpec((tr, d), lambda i: (i, 0))]
    row_block = pl.BlockSpec((tr, d), lambda i: (i, 0))
    out = pl.pallas_call(
        functools.partial(_norm_mod_kernel, d=d, n_latent_blocks=seq // tr, joins_streams=joins_streams),
        grid=(m // tr,),
        in_specs=stream_specs + [pl.BlockSpec((1, d), lambda i: (0, 0)),
                                 pl.BlockSpec((8, 3 * d), lambda i: (0, 0))],
        out_specs=(row_block, row_block) if joins_streams else row_block,
        out_shape=((jax.ShapeDtypeStruct((m, d), BF16), jax.ShapeDtypeStruct((m, d), F32))
                   if joins_streams else jax.ShapeDtypeStruct((m, d), BF16)),
        compiler_params=_params(("arbitrary",)),
        name="norm_mod",
    )(*operands, nw, mod)
    return out


def _matmul_kernel(a_ref, b_ref, o_ref):
    o_ref[...] = jnp.dot(a_ref[...], b_ref[...].astype(BF16), preferred_element_type=F32).astype(o_ref.dtype)


def _in_proj(h, w, layer, out_dtype):
    m, k = h.shape
    n = w.shape[2]
    tm = _pick(m, IN_PROJ_TM)
    tn = _pick(n, IN_PROJ_TN)
    return pl.pallas_call(
        _matmul_kernel,
        grid=(m // tm, n // tn),
        in_specs=[
            pl.BlockSpec((tm, k), lambda i, j: (i, 0)),
            pl.BlockSpec((None, k, tn), lambda i, j: (layer, 0, j)),
        ],
        out_specs=pl.BlockSpec((tm, tn), lambda i, j: (i, j)),
        out_shape=jax.ShapeDtypeStruct((m, n), out_dtype),
        compiler_params=_params(("arbitrary", "arbitrary")),
        name="in_proj",
    )(h, w)


def _retention_kernel(decf_ref, decb_ref, q_ref, k_ref, v_ref, g_ref, cos_ref, sin_ref, o_ref,
                      sf_ref, sb_ref, kvb_ref, lhs_ref, *, layer, n_latent_chunks, n_ctx_chunks):
    head = pl.program_id(0)
    lgf = jnp.log1p(-jnp.exp(jnp.full((1, LANES), decf_ref[layer, head], F32)))
    lgb = jnp.log1p(-jnp.exp(jnp.full((1, LANES), decb_ref[layer, head], F32)))
    ri = lax.broadcasted_iota(jnp.int32, (CHUNK, CHUNK), 0)
    ci = lax.broadcasted_iota(jnp.int32, (CHUNK, CHUNK), 1)
    rel = (ri - ci).astype(F32)
    pos = ri.astype(F32)
    dmat = jnp.where(rel >= 0, jnp.exp(lgf * jnp.maximum(rel, 0.0)), jnp.exp(lgb * jnp.maximum(-rel, 0.0)))
    wfk = jnp.exp(lgf * (CHUNK - 1.0 - pos))
    wbk = jnp.exp(lgb * pos)
    dqf = jnp.exp(lgf * (pos + 1.0))
    dqb = jnp.exp(lgb * (CHUNK - pos))
    cdf = jnp.exp(lgf * float(CHUNK))
    cdb = jnp.exp(lgb * float(CHUNK))
    ret_scale = HEAD_DIM ** -0.5

    def rows_of(c):
        return pl.ds(pl.multiple_of(c * CHUNK, CHUNK), CHUNK)

    def rope(ref, rows):
        x = ref[rows, :].astype(F32)
        return x * cos_ref[rows, :] + pltpu.roll(x, HEAD_DIM // 2, 1) * sin_ref[rows, :]

    def scan_states(first, n, sf, sb):
        def up(t, sf):
            c = first + t
            rows = rows_of(c)
            qr = rope(q_ref, rows)
            kr = rope(k_ref, rows) * ret_scale
            a = lax.dot_general(qr.astype(BF16), kr.astype(BF16), NT_DIMS, preferred_element_type=F32) * dmat
            lhs_ref[c] = jnp.concatenate(
                [a.astype(BF16), (qr * dqf).astype(BF16), (qr * dqb).astype(BF16)], axis=1)
            kk = jnp.concatenate([(kr * wfk).astype(BF16), (kr * wbk).astype(BF16)], axis=1)
            kv = lax.dot_general(kk, v_ref[rows, :].astype(BF16), TN_DIMS, preferred_element_type=F32)
            sf_ref[c] = sf.astype(BF16)
            kvb_ref[c] = kv[HEAD_DIM:, :]
            return sf * cdf + kv[:HEAD_DIM, :]

        sf = lax.fori_loop(0, n, up, sf, unroll=min(n, RET_UNROLL))

        def down(t, sb):
            c = first + n - 1 - t
            sb_ref[c] = sb.astype(BF16)
            return sb * cdb + kvb_ref[c]

        sb = lax.fori_loop(0, n, down, sb, unroll=min(n, RET_UNROLL))
        return sf, sb

    def outputs(first, n):
        def one(t, carry):
            c = first + t
            rows = rows_of(c)
            rhs = jnp.concatenate([v_ref[rows, :].astype(BF16), sf_ref[c], sb_ref[c]], axis=0)
            o = jnp.dot(lhs_ref[c], rhs, preferred_element_type=F32)
            o = o * lax.rsqrt(jnp.mean(o * o, axis=-1, keepdims=True) + EPS)
            o_ref[rows, :] = (o * _silu(g_ref[rows, :].astype(F32))).astype(o_ref.dtype)
            return carry

        lax.fori_loop(0, n, one, 0, unroll=min(n, RET_UNROLL))

    zero = jnp.zeros((HEAD_DIM, HEAD_DIM), F32)
    sf, sb = scan_states(n_latent_chunks, n_ctx_chunks, zero, zero)
    scan_states(0, n_latent_chunks, sf, sb)
    outputs(0, n_latent_chunks)
    outputs(n_latent_chunks, n_ctx_chunks)


def _retention(z, dec_f, dec_b, cos, sin, *, layer, heads, seq, q_blk, k_blk, v_blk, g_blk):
    m = z.shape[0]
    col = lambda base: pl.BlockSpec((m, HEAD_DIM), lambda h: (0, base + h))
    smem = pl.BlockSpec(memory_space=pltpu.SMEM)
    full = pl.BlockSpec((m, HEAD_DIM), lambda h: (0, 0), pipeline_mode=pl.Buffered(1))
    return pl.pallas_call(
        functools.partial(_retention_kernel, layer=layer, n_latent_chunks=seq // CHUNK,
                          n_ctx_chunks=(m - seq) // CHUNK),
        grid=(heads,),
        in_specs=[smem, smem, col(q_blk), col(k_blk), col(v_blk), col(g_blk), full, full],
        out_specs=pl.BlockSpec((m, HEAD_DIM), lambda h: (0, h)),
        out_shape=jax.ShapeDtypeStruct((m, heads * HEAD_DIM), BF16),
        scratch_shapes=[pltpu.VMEM((m // CHUNK, HEAD_DIM, HEAD_DIM), BF16),
                        pltpu.VMEM((m // CHUNK, HEAD_DIM, HEAD_DIM), BF16),
                        pltpu.VMEM((m // CHUNK, HEAD_DIM, HEAD_DIM), F32),
                        pltpu.VMEM((m // CHUNK, CHUNK, 3 * HEAD_DIM), BF16)],
        compiler_params=_params(("arbitrary",)),
        name="retention",
    )(dec_f, dec_b, z, z, z, z, cos, sin)


def _swap_half_pairs(x):
    lane = lax.broadcasted_iota(jnp.int32, x.shape, 1)
    return jnp.where(lane % 64 < 32, pltpu.roll(x, 96, 1), pltpu.roll(x, 32, 1))


def _dif_rope(x, cos, sin):
    return x * cos + _swap_half_pairs(x) * sin


def _attention_kernel(lq1_ref, lk1_ref, lq2_ref, lk2_ref, q_ref, cosq_ref, sinq_ref, qn_ref, cosn_ref, sinn_ref,
                      k_ref, v_ref, cosk_ref, sink_ref, gp_ref, nw_ref, o_ref,
                      ks_ref, vtx_ref, vtl_ref, s0_ref, s1_ref, cm_ref, acc_ref, *, lambda_init, seq, tq, tk):
    qi = pl.program_id(1)
    m = k_ref.shape[0]
    ctx = m - seq
    vrows = vtl_ref.shape[0]
    n_chunks = seq // tk
    last = n_chunks - 1
    ones_rows = (lax.broadcasted_iota(jnp.int32, (vrows - HEAD_DIM, 1), 0) == 0).astype(BF16)
    slots = (s0_ref, s1_ref)

    def chunk_len(c):
        return tk + ctx if c == last else tk

    @pl.when(qi == 0)
    def _():
        def prep_rows(rows, n):
            k = _dif_rope(k_ref[rows, :].astype(F32), cosk_ref[rows, :], sink_ref[rows, :])
            ks_ref[rows, :] = k.astype(BF16)
            vt = v_ref[rows, :].astype(F32).T.astype(BF16)
            return jnp.concatenate([vt, jnp.broadcast_to(ones_rows, (vrows - HEAD_DIM, n))], axis=0)

        def prep(c, carry):
            vtx_ref[c] = prep_rows(pl.ds(pl.multiple_of(c * tk, tk), tk), tk)
            return carry
        lax.fori_loop(0, last, prep, 0)
        vtl_ref[:, 0:tk] = prep_rows(pl.ds(last * tk, tk), tk)
        vtl_ref[:, tk:tk + ctx] = prep_rows(pl.ds(seq, ctx), ctx)

    def query_operands(ref, cos_ref, sin_ref):
        q = _dif_rope(ref[...].astype(F32), cos_ref[...], sin_ref[...]) * ((HEAD_DIM // 2) ** -0.5 * LOG2E)
        lane = lax.broadcasted_iota(jnp.int32, q.shape, 1)
        q0 = jnp.where(lane < HEAD_DIM // 2, q, 0.0).astype(BF16)
        q1 = jnp.where(lane >= HEAD_DIM // 2, q, 0.0).astype(BF16)
        return q0, q1

    def scores(qs, row0, n, slot, off):
        kb = ks_ref[pl.ds(row0, n), :]
        cmax = []
        for comp, qm in enumerate(qs):
            s = lax.dot_general(kb, qm, NT_DIMS, preferred_element_type=F32)
            slots[slot][comp, off:off + n, :] = s
            cmax.append(jnp.max(s, axis=0, keepdims=True))
        return tuple(cmax)

    def rescale(cmax, state):
        new = []
        for comp in range(2):
            mx, acc = state[comp]
            mn = jnp.maximum(mx, cmax[comp])
            new.append((mn, jnp.exp2(mx - mn) * acc))
        return tuple(new)

    def accumulate(slot, off, n, vt, state):
        new = []
        for comp in range(2):
            mn, acc = state[comp]
            p = jnp.exp2(slots[slot][comp, off:off + n, :] - mn).astype(BF16)
            new.append((mn, acc + jnp.dot(vt, p, preferred_element_type=F32)))
        return tuple(new)

    def parts(c):
        step = tk // N_PARTS
        out = [(j * step, step) for j in range(N_PARTS)]
        if c == last:
            out[-1] = (out[-1][0], step + ctx)
        return out

    def vt_part(c, off, n):
        return vtl_ref[:, off:off + n] if c == last else vtx_ref[c, :, off:off + n]

    def finalize_previous():
        acc0 = acc_ref[0]
        acc1 = acc_ref[1]
        lam = (jnp.exp(jnp.sum(lq1_ref[...] * lk1_ref[...], axis=-1, keepdims=True))
               - jnp.exp(jnp.sum(lq2_ref[...] * lk2_ref[...], axis=-1, keepdims=True)) + lambda_init)
        l0 = acc0[HEAD_DIM:HEAD_DIM + 1, :]
        l1 = acc1[HEAD_DIM:HEAD_DIM + 1, :]
        ot = acc0[:HEAD_DIM, :] * (1.0 / l0) - lam * (acc1[:HEAD_DIM, :] * (1.0 / l1))
        o = ot.T
        o = o * lax.rsqrt(jnp.mean(o * o, axis=-1, keepdims=True) + EPS) * nw_ref[...] * (1.0 - lambda_init)
        o_ref[...] = (o * _silu(gp_ref[...].astype(F32))).astype(o_ref.dtype)

    def leave_sums(state):
        (_, acc0), (_, acc1) = state
        acc_ref[0] = acc0
        acc_ref[1] = acc1

    init = ((jnp.full((1, tq), NEG_BIG, F32), jnp.zeros((vrows, tq), F32)),) * 2
    qs = query_operands(q_ref, cosq_ref, sinq_ref)
    n_latent_blocks = seq // tq

    @pl.when(qi == 0)
    def _():
        cmax = scores(qs, 0, chunk_len(0), 0, 0)
        cm_ref[0] = cmax[0]
        cm_ref[1] = cmax[1]
        acc_ref[...] = jnp.ones(acc_ref.shape, F32)

    @pl.when(qi < n_latent_blocks)
    def _():
        finalize_previous()
        cmax = (cm_ref[0], cm_ref[1])
        state = init
        for c in range(n_chunks):
            state = rescale(cmax, state)
            nc = (c + 1) % n_chunks
            nqs = qs if c < last else query_operands(qn_ref, cosn_ref, sinn_ref)
            nxt = None
            for (off, n), (noff, nn) in zip(parts(c), parts(nc)):
                part = scores(nqs, nc * tk + noff, nn, nc % 2, noff)
                nxt = part if nxt is None else tuple(jnp.maximum(a, b) for a, b in zip(nxt, part))
                state = accumulate(c % 2, off, n, vt_part(c, off, n), state)
            cmax = nxt
        cm_ref[0] = cmax[0]
        cm_ref[1] = cmax[1]
        leave_sums(state)

    @pl.when(qi == n_latent_blocks)
    def _():
        finalize_previous()
        cmax = scores(qs, seq, ctx, 1, 0)
        leave_sums(accumulate(1, 0, ctx, vtl_ref[:, tk:tk + ctx], rescale(cmax, init)))

    @pl.when(qi > n_latent_blocks)
    def _():
        finalize_previous()


def _attention(z, lams, cos, sin, nw, *, lambda_init, heads, seq, q_blk, k_blk, v_blk, g_blk):
    m = z.shape[0]
    ctx = m - seq
    tq = ATTN_TQ
    n_chunks = ATTN_CHUNKS
    tk = seq // n_chunks
    assert seq % tq == 0 and ctx == tq, "one query block must hold exactly the context rows"
    assert seq % (n_chunks * LANES) == 0 and n_chunks % 2 == 0 and n_chunks >= 2 and tk % N_PARTS == 0
    nq = m // tq
    vrows = HEAD_DIM + BF16_SUBLANES
    lam_spec = pl.BlockSpec((1, HEAD_DIM // 2), lambda h, i: (0, 0))
    this_row = lambda base, per_head: pl.BlockSpec(
        (tq, HEAD_DIM), lambda h, i: (jnp.minimum(i, nq - 1), base + h * per_head))
    next_row = lambda base, per_head: pl.BlockSpec(
        (tq, HEAD_DIM), lambda h, i: (jnp.minimum(i + 1, nq - 1), base + h * per_head))
    prev_row = lambda base: pl.BlockSpec((tq, HEAD_DIM), lambda h, i: (jnp.maximum(i - 1, 0), base + h))
    kcol = lambda base: pl.BlockSpec((m, HEAD_DIM), lambda h, i: (0, base + h))
    table = pl.BlockSpec((m, HEAD_DIM), lambda h, i: (0, 0), pipeline_mode=pl.Buffered(1))
    return pl.pallas_call(
        functools.partial(_attention_kernel, lambda_init=lambda_init, seq=seq, tq=tq, tk=tk),
        grid=(heads, nq + 1),
        in_specs=[lam_spec, lam_spec, lam_spec, lam_spec,
                  this_row(q_blk, 1), this_row(0, 0), this_row(0, 0),
                  next_row(q_blk, 1), next_row(0, 0), next_row(0, 0),
                  kcol(k_blk), kcol(v_blk), table, table,
                  prev_row(g_blk),
                  pl.BlockSpec((1, HEAD_DIM), lambda h, i: (0, 0))],
        out_specs=prev_row(0),
        out_shape=jax.ShapeDtypeStruct((m, heads * HEAD_DIM), BF16),
        scratch_shapes=[pltpu.VMEM((m, HEAD_DIM), BF16),
                        pltpu.VMEM((n_chunks - 1, vrows, tk), BF16),
                        pltpu.VMEM((vrows, tk + ctx), BF16),
                        pltpu.VMEM((2, tk + ctx, tq), F32),
                        pltpu.VMEM((2, tk + ctx, tq), F32),
                        pltpu.VMEM((2, 1, tq), F32),
                        pltpu.VMEM((2, vrows, tq), F32)],
        compiler_params=_params(("arbitrary", "arbitrary")),
        name="diff_attention",
    )(*lams, z, cos, sin, z, cos, sin, z, z, cos, sin, z, nw)


def _mlp_kernel(u_ref, v_ref, g_ref, vnw_ref, ws_ref, bst_ref, o_ref, *, groups):
    tr, width = u_ref.shape
    gd = width // groups
    v = _gelu(v_ref[...].astype(F32))
    v = (v * lax.rsqrt(jnp.mean(v * v, axis=-1, keepdims=True) + EPS) * vnw_ref[...]).astype(BF16)
    for c in range(tr // CHUNK):
        rows = slice(c * CHUNK, (c + 1) * CHUNK)
        for g in range(groups):
            cols = slice(g * gd, (g + 1) * gd)
            mixed = jnp.dot(ws_ref[g].astype(BF16), v[rows, cols], preferred_element_type=F32) + bst_ref[:, g:g + 1]
            u = _gelu(u_ref[rows, cols].astype(F32))
            o_ref[rows, cols] = (u * mixed * _silu(g_ref[rows, cols].astype(F32))).astype(o_ref.dtype)


def _chunk_mlp(z, vnw, ws, bst, *, u_blk, width):
    m = z.shape[0]
    groups = ws.shape[0]
    tr = _pick(m, WIDE_ROW_TILES)
    col = lambda b: pl.BlockSpec((tr, width), lambda i: (i, b))
    return pl.pallas_call(
        functools.partial(_mlp_kernel, groups=groups),
        grid=(m // tr,),
        in_specs=[col(u_blk), col(u_blk + 1), col(u_blk + 2),
                  pl.BlockSpec((1, width), lambda i: (0, 0)),
                  pl.BlockSpec((groups, CHUNK, CHUNK), lambda i: (0, 0, 0)),
                  pl.BlockSpec((CHUNK, groups), lambda i: (0, 0))],
        out_specs=pl.BlockSpec((tr, width), lambda i: (i, 0)),
        out_shape=jax.ShapeDtypeStruct((m, width), BF16),
        compiler_params=_params(("arbitrary",)),
        name="chunk_mlp",
    )(z, z, z, vnw, ws, bst)


def _out_proj_kernel(ret_ref, dif_ref, mlp_ref, wr_ref, wd_ref, wm_ref, x_ref, gate_ref, o_ref, *, seq, tm):
    y = jnp.dot(ret_ref[...], wr_ref[...].astype(BF16), preferred_element_type=F32)
    y = y + jnp.dot(dif_ref[...], wd_ref[...].astype(BF16), preferred_element_type=F32)
    y = y + jnp.dot(mlp_ref[...], wm_ref[...].astype(BF16), preferred_element_type=F32)
    row = pl.program_id(0) * tm + lax.broadcasted_iota(jnp.int32, y.shape, 0)
    gate = jnp.where(row >= seq, gate_ref[1:2, :], gate_ref[0:1, :])
    o_ref[...] = x_ref[...] + gate * y


def _out_proj(ret, dif, mlp, w_out, layer, xx, mod, *, seq):
    m, d = xx.shape
    rw, dw, mw = ret.shape[1], dif.shape[1], mlp.shape[1]
    assert rw == dw and (rw + dw) % mw == 0
    tm = _pick(m, OUT_PROJ_TM)
    tn = _pick(d, OUT_PROJ_TN)
    gate_blk = 2 * d // tn
    return pl.pallas_call(
        functools.partial(_out_proj_kernel, seq=seq, tm=tm),
        grid=(m // tm, d // tn),
        in_specs=[
            pl.BlockSpec((tm, rw), lambda i, j: (i, 0)),
            pl.BlockSpec((tm, dw), lambda i, j: (i, 0)),
            pl.BlockSpec((tm, mw), lambda i, j: (i, 0)),
            pl.BlockSpec((None, rw, tn), lambda i, j: (layer, 0, j)),
            pl.BlockSpec((None, dw, tn), lambda i, j: (layer, 1, j)),
            pl.BlockSpec((None, mw, tn), lambda i, j: (layer, (rw + dw) // mw, j)),
            pl.BlockSpec((tm, tn), lambda i, j: (i, j)),
            pl.BlockSpec((8, tn), lambda i, j: (0, gate_blk + j)),
        ],
        out_specs=pl.BlockSpec((tm, tn), lambda i, j: (i, j)),
        out_shape=jax.ShapeDtypeStruct((m, d), F32),
        compiler_params=_params(("arbitrary", "arbitrary")),
        name="out_proj",
    )(ret, dif, mlp, w_out, w_out, w_out, xx, mod)


def _final_norm_kernel(x_ref, w_ref, o_ref):
    x = x_ref[...]
    o_ref[...] = x * lax.rsqrt(jnp.mean(x * x, axis=-1, keepdims=True) + EPS) * w_ref[...]


def _final_norm(xx, w, seq):
    d = xx.shape[1]
    tr = _pick(seq, WIDE_ROW_TILES)
    return pl.pallas_call(
        _final_norm_kernel,
        grid=(seq // tr,),
        in_specs=[pl.BlockSpec((tr, d), lambda i: (i, 0)), pl.BlockSpec((1, d), lambda i: (0, 0))],
        out_specs=pl.BlockSpec((tr, d), lambda i: (i, 0)),
        out_shape=jax.ShapeDtypeStruct((seq, d), F32),
        compiler_params=_params(("arbitrary",)),
        name="final_norm",
    )(xx, w)


def _rope_tables(seq, ctx, dim, repeats):
    rows = seq // GRID_W
    n_freq = dim // 4
    inv = ROPE_BASE ** (-jnp.arange(n_freq, dtype=F32) / n_freq)
    ang_r = jnp.arange(rows, dtype=F32)[:, None] * inv
    ang_c = jnp.arange(GRID_W, dtype=F32)[:, None] * inv

    def per_token(f):
        by_row = jnp.broadcast_to(f(ang_r)[:, None, :], (rows, GRID_W, n_freq))
        by_col = jnp.broadcast_to(f(ang_c)[None, :, :], (rows, GRID_W, n_freq))
        return jnp.concatenate([by_row, by_col], axis=-1).reshape(seq, 2 * n_freq)

    cos_t, sin_t = per_token(jnp.cos), per_token(jnp.sin)
    cos = jnp.concatenate([cos_t, cos_t], axis=-1)
    sin = jnp.concatenate([-sin_t, sin_t], axis=-1)
    cos = jnp.concatenate([jnp.tile(cos, (1, repeats)), jnp.ones((ctx, repeats * dim), F32)], axis=0)
    sin = jnp.concatenate([jnp.tile(sin, (1, repeats)), jnp.zeros((ctx, repeats * dim), F32)], axis=0)
    return cos, sin


def kernel(x, c, ctx, c_ctx, w_in, w_out, w_ada, b_ada, norm_w, ret_decay_fwd, ret_decay_bwd,
           lam_q1, lam_k1, lam_q2, lam_k2, dif_norm_w, mlp_vnorm_w, mlp_ws, mlp_bs, final_norm_w):
    batch, seq, d = x.shape
    assert batch == 1, "the kernels process one sequence"
    n_ctx = ctx.shape[1]
    depth = w_in.shape[0]
    heads = ret_decay_fwd.shape[1]
    ret_w = heads * HEAD_DIM
    dif_w = ret_w
    mlp_w = mlp_vnorm_w.shape[1]
    assert 2 * ret_w + mlp_w == d and w_in.shape[2] == 8 * ret_w + 3 * mlp_w

    rk_blk, rv_blk, dk_blk, dv_blk = 0, heads, 2 * heads, 3 * heads
    rq_blk, dq_blk, rg_blk, dg_blk = 4 * heads, 5 * heads, 6 * heads, 7 * heads
    mu_blk = 8 * ret_w // mlp_w

    xx = (x[0], ctx[0])
    cc = jnp.zeros((8, d), F32).at[0].set(c[0]).at[1].set(c_ctx)
    mods = _modulation(cc, w_ada, b_ada)
    cos_r, sin_r = _rope_tables(seq, n_ctx, HEAD_DIM, 1)
    cos_d, sin_d = _rope_tables(seq, n_ctx, HEAD_DIM // 2, 2)

    for l in range(depth):
        lambda_init = 0.8 - 0.6 * math.exp(-0.3 * l)
        h = _norm_mod(xx, norm_w[l][None, :], mods[l], seq)
        if l == 0:
            h, xx = h
        z = _in_proj(h, w_in, l, BF16)
        ret = _retention(z, ret_decay_fwd, ret_decay_bwd, cos_r, sin_r, layer=l, heads=heads, seq=seq,
                         q_blk=rq_blk, k_blk=rk_blk, v_blk=rv_blk, g_blk=rg_blk)
        lams = [a[l][None, :] for a in (lam_q1, lam_k1, lam_q2, lam_k2)]
        dif = _attention(z, lams, cos_d, sin_d, dif_norm_w[l][None, :], lambda_init=lambda_init,
                         heads=heads, seq=seq, q_blk=dq_blk, k_blk=dk_blk, v_blk=dv_blk, g_blk=dg_blk)
        mlp = _chunk_mlp(z, mlp_vnorm_w[l][None, :], mlp_ws[l], mlp_bs[l].T, u_blk=mu_blk, width=mlp_w)
        xx = _out_proj(ret, dif, mlp, w_out, l, xx, mods[l], seq=seq)

    return _final_norm(xx, final_norm_w[None, :], seq)[None]
```

```python
import functools
import math

import jax
import jax.numpy as jnp
from jax import lax
from jax.experimental import pallas as pl
from jax.experimental.pallas import tpu as pltpu

HEAD_DIM = 128
CHUNK = 128
GRID_W = 64
ROPE_BASE = 10000.0
EPS = 1e-6
LOG2E = 1.4426950408889634
NEG_BIG = -1e30

LANES = 128
BF16_SUBLANES = 16
VMEM_BYTES = 64 * 1024 * 1024
VMEM_LIMIT = VMEM_BYTES - 8 * 1024 * 1024

ROW_TILE = 256
WIDE_ROW_TILES = (768, 512, 256)
MOD_TN = (512, 256, 128)
IN_PROJ_TM = (1408, 1056, 768, 640, 256)
IN_PROJ_TN = (512, 256)
OUT_PROJ_TM = (1056, 768, 640, 256)
OUT_PROJ_TN = (512, 256)
ATTN_TQ = 256
ATTN_CHUNKS = 4
N_PARTS = 4
RET_UNROLL = 8

BF16 = jnp.bfloat16
F32 = jnp.float32

NT_DIMS = (((1,), (1,)), ((), ()))
TN_DIMS = (((0,), (0,)), ((), ()))


def _params(semantics):
    return pltpu.CompilerParams(dimension_semantics=semantics, vmem_limit_bytes=VMEM_LIMIT)


def _pick(n, candidates):
    for c in candidates:
        if n % c == 0:
            return c
    return n


def _silu(x):
    return x * (1.0 / (1.0 + jnp.exp(-x)))


def _gelu(x):
    return 0.5 * x * (1.0 + lax.erf(x * (2.0 ** -0.5)))


def _mod_kernel(cc_ref, w_ref, b_ref, o_ref):
    a = _silu(cc_ref[...]).astype(BF16)
    o_ref[...] = jnp.dot(a, w_ref[...].astype(BF16), preferred_element_type=F32) + b_ref[...]


def _modulation(cc, w_ada, b_ada):
    depth, d, n3 = w_ada.shape
    tn = _pick(n3, MOD_TN)
    return pl.pallas_call(
        _mod_kernel,
        grid=(depth, n3 // tn),
        in_specs=[
            pl.BlockSpec((8, d), lambda l, j: (0, 0)),
            pl.BlockSpec((None, d, tn), lambda l, j: (l, 0, j)),
            pl.BlockSpec((None, 1, tn), lambda l, j: (l, 0, j)),
        ],
        out_specs=pl.BlockSpec((None, 8, tn), lambda l, j: (l, 0, j)),
        out_shape=jax.ShapeDtypeStruct((depth, 8, n3), F32),
        compiler_params=_params(("arbitrary", "arbitrary")),
        name="modulation",
    )(cc, w_ada, b_ada.reshape(depth, 1, n3))


def _norm_mod_kernel(*refs, d, n_latent_blocks, joins_streams):
    is_ctx = pl.program_id(0) >= n_latent_blocks
    if joins_streams:
        x_ref, c_ref, nw_ref, mod_ref, h_ref, xx_ref = refs
        x = jnp.where(is_ctx, c_ref[...], x_ref[...])
        xx_ref[...] = x
    else:
        x_ref, nw_ref, mod_ref, h_ref = refs
        x = x_ref[...]
    y = x * lax.rsqrt(jnp.mean(x * x, axis=-1, keepdims=True) + EPS) * nw_ref[...]
    shift = jnp.where(is_ctx, mod_ref[1:2, 0:d], mod_ref[0:1, 0:d])
    scale = jnp.where(is_ctx, mod_ref[1:2, d:2 * d], mod_ref[0:1, d:2 * d])
    h_ref[...] = (y * (1.0 + scale) + shift).astype(h_ref.dtype)


def _norm_mod(streams, nw, mod, seq):
    joins_streams = isinstance(streams, tuple)
    tr = ROW_TILE
    if joins_streams:
        x, c = streams
        d = x.shape[1]
        m = seq + c.shape[0]
        n_latent_blocks = seq // tr
        operands = (x, c)
        stream_specs = [pl.BlockSpec((tr, d), lambda i: (jnp.minimum(i, n_latent_blocks - 1), 0)),
                        pl.BlockSpec((tr, d), lambda i: (jnp.maximum(i - n_latent_blocks, 0), 0))]
    else:
        m, d = streams.shape
        operands = (streams,)
        stream_specs = [pl.BlockSpec((tr, d), lambda i: (i, 0))]
    row_block = pl.BlockSpec((tr, d), lambda i: (i, 0))
    out = pl.pallas_call(
        functools.partial(_norm_mod_kernel, d=d, n_latent_blocks=seq // tr, joins_streams=joins_streams),
        grid=(m // tr,),
        in_specs=stream_specs + [pl.BlockSpec((1, d), lambda i: (0, 0)),
                                 pl.BlockSpec((8, 3 * d), lambda i: (0, 0))],
        out_specs=(row_block, row_block) if joins_streams else row_block,
        out_shape=((jax.ShapeDtypeStruct((m, d), BF16), jax.ShapeDtypeStruct((m, d), F32))
                   if joins_streams else jax.ShapeDtypeStruct((m, d), BF16)),
        compiler_params=_params(("arbitrary",)),
        name="norm_mod",
    )(*operands, nw, mod)
    return out


def _matmul_kernel(a_ref, b_ref, o_ref):
    o_ref[...] = jnp.dot(a_ref[...], b_ref[...].astype(BF16), preferred_element_type=F32).astype(o_ref.dtype)


def _in_proj(h, w, layer, out_dtype):
    m, k = h.shape
    n = w.shape[2]
    tm = _pick(m, IN_PROJ_TM)
    tn = _pick(n, IN_PROJ_TN)
    return pl.pallas_call(
        _matmul_kernel,
        grid=(m // tm, n // tn),
        in_specs=[
            pl.BlockSpec((tm, k), lambda i, j: (i, 0)),
            pl.BlockSpec((None, k, tn), lambda i, j: (layer, 0, j)),
        ],
        out_specs=pl.BlockSpec((tm, tn), lambda i, j: (i, j)),
        out_shape=jax.ShapeDtypeStruct((m, n), out_dtype),
        compiler_params=_params(("arbitrary", "arbitrary")),
        name="in_proj",
    )(h, w)


def _retention_kernel(decf_ref, decb_ref, q_ref, k_ref, v_ref, g_ref, cos_ref, sin_ref, o_ref,
                      sf_ref, sb_ref, kvb_ref, lhs_ref, *, layer, n_latent_chunks, n_ctx_chunks):
    head = pl.program_id(0)
    lgf = jnp.log1p(-jnp.exp(jnp.full((1, LANES), decf_ref[layer, head], F32)))
    lgb = jnp.log1p(-jnp.exp(jnp.full((1, LANES), decb_ref[layer, head], F32)))
    ri = lax.broadcasted_iota(jnp.int32, (CHUNK, CHUNK), 0)
    ci = lax.broadcasted_iota(jnp.int32, (CHUNK, CHUNK), 1)
    rel = (ri - ci).astype(F32)
    pos = ri.astype(F32)
    dmat = jnp.where(rel >= 0, jnp.exp(lgf * jnp.maximum(rel, 0.0)), jnp.exp(lgb * jnp.maximum(-rel, 0.0)))
    wfk = jnp.exp(lgf * (CHUNK - 1.0 - pos))
    wbk = jnp.exp(lgb * pos)
    dqf = jnp.exp(lgf * (pos + 1.0))
    dqb = jnp.exp(lgb * (CHUNK - pos))
    cdf = jnp.exp(lgf * float(CHUNK))
    cdb = jnp.exp(lgb * float(CHUNK))
    ret_scale = HEAD_DIM ** -0.5

    def rows_of(c):
        return pl.ds(pl.multiple_of(c * CHUNK, CHUNK), CHUNK)

    def rope(ref, rows):
        x = ref[rows, :].astype(F32)
        return x * cos_ref[rows, :] + pltpu.roll(x, HEAD_DIM // 2, 1) * sin_ref[rows, :]

    def scan_states(first, n, sf, sb):
        def up(t, sf):
            c = first + t
            rows = rows_of(c)
            qr = rope(q_ref, rows)
            kr = rope(k_ref, rows) * ret_scale
            a = lax.dot_general(qr.astype(BF16), kr.astype(BF16), NT_DIMS, preferred_element_type=F32) * dmat
            lhs_ref[c] = jnp.concatenate(
                [a.astype(BF16), (qr * dqf).astype(BF16), (qr * dqb).astype(BF16)], axis=1)
            kk = jnp.concatenate([(kr * wfk).astype(BF16), (kr * wbk).astype(BF16)], axis=1)
            kv = lax.dot_general(kk, v_ref[rows, :].astype(BF16), TN_DIMS, preferred_element_type=F32)
            sf_ref[c] = sf.astype(BF16)
            kvb_ref[c] = kv[HEAD_DIM:, :]
            return sf * cdf + kv[:HEAD_DIM, :]

        sf = lax.fori_loop(0, n, up, sf, unroll=min(n, RET_UNROLL))

        def down(t, sb):
            c = first + n - 1 - t
            sb_ref[c] = sb.astype(BF16)
            return sb * cdb + kvb_ref[c]

        sb = lax.fori_loop(0, n, down, sb, unroll=min(n, RET_UNROLL))
        return sf, sb

    def outputs(first, n):
        def one(t, carry):
            c = first + t
            rows = rows_of(c)
            rhs = jnp.concatenate([v_ref[rows, :].astype(BF16), sf_ref[c], sb_ref[c]], axis=0)
            o = jnp.dot(lhs_ref[c], rhs, preferred_element_type=F32)
            o = o * lax.rsqrt(jnp.mean(o * o, axis=-1, keepdims=True) + EPS)
            o_ref[rows, :] = (o * _silu(g_ref[rows, :].astype(F32))).astype(o_ref.dtype)
            return carry

        lax.fori_loop(0, n, one, 0, unroll=min(n, RET_UNROLL))

    zero = jnp.zeros((HEAD_DIM, HEAD_DIM), F32)
    sf, sb = scan_states(n_latent_chunks, n_ctx_chunks, zero, zero)
    scan_states(0, n_latent_chunks, sf, sb)
    outputs(0, n_latent_chunks)
    outputs(n_latent_chunks, n_ctx_chunks)


def _retention(z, dec_f, dec_b, cos, sin, *, layer, heads, seq, q_blk, k_blk, v_blk, g_blk):
    m = z.shape[0]
    col = lambda base: pl.BlockSpec((m, HEAD_DIM), lambda h: (0, base + h))
    smem = pl.BlockSpec(memory_space=pltpu.SMEM)
    full = pl.BlockSpec((m, HEAD_DIM), lambda h: (0, 0), pipeline_mode=pl.Buffered(1))
    return pl.pallas_call(
        functools.partial(_retention_kernel, layer=layer, n_latent_chunks=seq // CHUNK,
                          n_ctx_chunks=(m - seq) // CHUNK),
        grid=(heads,),
        in_specs=[smem, smem, col(q_blk), col(k_blk), col(v_blk), col(g_blk), full, full],
        out_specs=pl.BlockSpec((m, HEAD_DIM), lambda h: (0, h)),
        out_shape=jax.ShapeDtypeStruct((m, heads * HEAD_DIM), BF16),
        scratch_shapes=[pltpu.VMEM((m // CHUNK, HEAD_DIM, HEAD_DIM), BF16),
                        pltpu.VMEM((m // CHUNK, HEAD_DIM, HEAD_DIM), BF16),
                        pltpu.VMEM((m // CHUNK, HEAD_DIM, HEAD_DIM), F32),
                        pltpu.VMEM((m // CHUNK, CHUNK, 3 * HEAD_DIM), BF16)],
        compiler_params=_params(("arbitrary",)),
        name="retention",
    )(dec_f, dec_b, z, z, z, z, cos, sin)


def _swap_half_pairs(x):
    lane = lax.broadcasted_iota(jnp.int32, x.shape, 1)
    return jnp.where(lane % 64 < 32, pltpu.roll(x, 96, 1), pltpu.roll(x, 32, 1))


def _dif_rope(x, cos, sin):
    return x * cos + _swap_half_pairs(x) * sin


def _attention_kernel(lq1_ref, lk1_ref, lq2_ref, lk2_ref, q_ref, cosq_ref, sinq_ref, qn_ref, cosn_ref, sinn_ref,
                      k_ref, v_ref, cosk_ref, sink_ref, gp_ref, nw_ref, o_ref,
                      ks_ref, vtx_ref, vtl_ref, s0_ref, s1_ref, cm_ref, acc_ref, *, lambda_init, seq, tq, tk):
    qi = pl.program_id(1)
    m = k_ref.shape[0]
    ctx = m - seq
    vrows = vtl_ref.shape[0]
    n_chunks = seq // tk
    last = n_chunks - 1
    ones_rows = (lax.broadcasted_iota(jnp.int32, (vrows - HEAD_DIM, 1), 0) == 0).astype(BF16)
    slots = (s0_ref, s1_ref)

    def chunk_len(c):
        return tk + ctx if c == last else tk

    @pl.when(qi == 0)
    def _():
        def prep_rows(rows, n):
            k = _dif_rope(k_ref[rows, :].astype(F32), cosk_ref[rows, :], sink_ref[rows, :])
            ks_ref[rows, :] = k.astype(BF16)
            vt = v_ref[rows, :].astype(F32).T.astype(BF16)
            return jnp.concatenate([vt, jnp.broadcast_to(ones_rows, (vrows - HEAD_DIM, n))], axis=0)

        def prep(c, carry):
            vtx_ref[c] = prep_rows(pl.ds(pl.multiple_of(c * tk, tk), tk), tk)
            return carry
        lax.fori_loop(0, last, prep, 0)
        vtl_ref[:, 0:tk] = prep_rows(pl.ds(last * tk, tk), tk)
        vtl_ref[:, tk:tk + ctx] = prep_rows(pl.ds(seq, ctx), ctx)

    def query_operands(ref, cos_ref, sin_ref):
        q = _dif_rope(ref[...].astype(F32), cos_ref[...], sin_ref[...]) * ((HEAD_DIM // 2) ** -0.5 * LOG2E)
        lane = lax.broadcasted_iota(jnp.int32, q.shape, 1)
        q0 = jnp.where(lane < HEAD_DIM // 2, q, 0.0).astype(BF16)
        q1 = jnp.where(lane >= HEAD_DIM // 2, q, 0.0).astype(BF16)
        return q0, q1

    def scores(qs, row0, n, slot, off):
        kb = ks_ref[pl.ds(row0, n), :]
        cmax = []
        for comp, qm in enumerate(qs):
            s = lax.dot_general(kb, qm, NT_DIMS, preferred_element_type=F32)
            slots[slot][comp, off:off + n, :] = s
            cmax.append(jnp.max(s, axis=0, keepdims=True))
        return tuple(cmax)

    def rescale(cmax, state):
        new = []
        for comp in range(2):
            mx, acc = state[comp]
            mn = jnp.maximum(mx, cmax[comp])
            new.append((mn, jnp.exp2(mx - mn) * acc))
        return tuple(new)

    def accumulate(slot, off, n, vt, state):
        new = []
        for comp in range(2):
            mn, acc = state[comp]
            p = jnp.exp2(slots[slot][comp, off:off + n, :] - mn).astype(BF16)
            new.append((mn, acc + jnp.dot(vt, p, preferred_element_type=F32)))
        return tuple(new)

    def parts(c):
        step = tk // N_PARTS
        out = [(j * step, step) for j in range(N_PARTS)]
        if c == last:
            out[-1] = (out[-1][0], step + ctx)
        return out

    def vt_part(c, off, n):
        return vtl_ref[:, off:off + n] if c == last else vtx_ref[c, :, off:off + n]

    def finalize_previous():
        acc0 = acc_ref[0]
        acc1 = acc_ref[1]
        lam = (jnp.exp(jnp.sum(lq1_ref[...] * lk1_ref[...], axis=-1, keepdims=True))
               - jnp.exp(jnp.sum(lq2_ref[...] * lk2_ref[...], axis=-1, keepdims=True)) + lambda_init)
        l0 = acc0[HEAD_DIM:HEAD_DIM + 1, :]
        l1 = acc1[HEAD_DIM:HEAD_DIM + 1, :]
        ot = acc0[:HEAD_DIM, :] * (1.0 / l0) - lam * (acc1[:HEAD_DIM, :] * (1.0 / l1))
        o = ot.T
        o = o * lax.rsqrt(jnp.mean(o * o, axis=-1, keepdims=True) + EPS) * nw_ref[...] * (1.0 - lambda_init)
        o_ref[...] = (o * _silu(gp_ref[...].astype(F32))).astype(o_ref.dtype)

    def leave_sums(state):
        (_, acc0), (_, acc1) = state
        acc_ref[0] = acc0
        acc_ref[1] = acc1

    init = ((jnp.full((1, tq), NEG_BIG, F32), jnp.zeros((vrows, tq), F32)),) * 2
    qs = query_operands(q_ref, cosq_ref, sinq_ref)
    n_latent_blocks = seq // tq

    @pl.when(qi == 0)
    def _():
        cmax = scores(qs, 0, chunk_len(0), 0, 0)
        cm_ref[0] = cmax[0]
        cm_ref[1] = cmax[1]
        acc_ref[...] = jnp.ones(acc_ref.shape, F32)

    @pl.when(qi < n_latent_blocks)
    def _():
        finalize_previous()
        cmax = (cm_ref[0], cm_ref[1])
        state = init
        for c in range(n_chunks):
            state = rescale(cmax, state)
            nc = (c + 1) % n_chunks
            nqs = qs if c < last else query_operands(qn_ref, cosn_ref, sinn_ref)
            nxt = None
            for (off, n), (noff, nn) in zip(parts(c), parts(nc)):
                part = scores(nqs, nc * tk + noff, nn, nc % 2, noff)
                nxt = part if nxt is None else tuple(jnp.maximum(a, b) for a, b in zip(nxt, part))
                state = accumulate(c % 2, off, n, vt_part(c, off, n), state)
            cmax = nxt
        cm_ref[0] = cmax[0]
        cm_ref[1] = cmax[1]
        leave_sums(state)

    @pl.when(qi == n_latent_blocks)
    def _():
        finalize_previous()
        cmax = scores(qs, seq, ctx, 1, 0)
        leave_sums(accumulate(1, 0, ctx, vtl_ref[:, tk:tk + ctx], rescale(cmax, init)))

    @pl.when(qi > n_latent_blocks)
    def _():
        finalize_previous()


def _attention(z, lams, cos, sin, nw, *, lambda_init, heads, seq, q_blk, k_blk, v_blk, g_blk):
    m = z.shape[0]
    ctx = m - seq
    tq = ATTN_TQ
    n_chunks = ATTN_CHUNKS
    tk = seq // n_chunks
    assert seq % tq == 0 and ctx == tq, "one query block must hold exactly the context rows"
    assert seq % (n_chunks * LANES) == 0 and n_chunks % 2 == 0 and n_chunks >= 2 and tk % N_PARTS == 0
    nq = m // tq
    vrows = HEAD_DIM + BF16_SUBLANES
    lam_spec = pl.BlockSpec((1, HEAD_DIM // 2), lambda h, i: (0, 0))
    this_row = lambda base, per_head: pl.BlockSpec(
        (tq, HEAD_DIM), lambda h, i: (jnp.minimum(i, nq - 1), base + h * per_head))
    next_row = lambda base, per_head: pl.BlockSpec(
        (tq, HEAD_DIM), lambda h, i: (jnp.minimum(i + 1, nq - 1), base + h * per_head))
    prev_row = lambda base: pl.BlockSpec((tq, HEAD_DIM), lambda h, i: (jnp.maximum(i - 1, 0), base + h))
    kcol = lambda base: pl.BlockSpec((m, HEAD_DIM), lambda h, i: (0, base + h))
    table = pl.BlockSpec((m, HEAD_DIM), lambda h, i: (0, 0), pipeline_mode=pl.Buffered(1))
    return pl.pallas_call(
        functools.partial(_attention_kernel, lambda_init=lambda_init, seq=seq, tq=tq, tk=tk),
        grid=(heads, nq + 1),
        in_specs=[lam_spec, lam_spec, lam_spec, lam_spec,
                  this_row(q_blk, 1), this_row(0, 0), this_row(0, 0),
                  next_row(q_blk, 1), next_row(0, 0), next_row(0, 0),
                  kcol(k_blk), kcol(v_blk), table, table,
                  prev_row(g_blk),
                  pl.BlockSpec((1, HEAD_DIM), lambda h, i: (0, 0))],
        out_specs=prev_row(0),
        out_shape=jax.ShapeDtypeStruct((m, heads * HEAD_DIM), BF16),
        scratch_shapes=[pltpu.VMEM((m, HEAD_DIM), BF16),
                        pltpu.VMEM((n_chunks - 1, vrows, tk), BF16),
                        pltpu.VMEM((vrows, tk + ctx), BF16),
                        pltpu.VMEM((2, tk + ctx, tq), F32),
                        pltpu.VMEM((2, tk + ctx, tq), F32),
                        pltpu.VMEM((2, 1, tq), F32),
                        pltpu.VMEM((2, vrows, tq), F32)],
        compiler_params=_params(("arbitrary", "arbitrary")),
        name="diff_attention",
    )(*lams, z, cos, sin, z, cos, sin, z, z, cos, sin, z, nw)


def _mlp_kernel(u_ref, v_ref, g_ref, vnw_ref, ws_ref, bst_ref, o_ref, *, groups):
    tr, width = u_ref.shape
    gd = width // groups
    v = _gelu(v_ref[...].astype(F32))
    v = (v * lax.rsqrt(jnp.mean(v * v, axis=-1, keepdims=True) + EPS) * vnw_ref[...]).astype(BF16)
    for c in range(tr // CHUNK):
        rows = slice(c * CHUNK, (c + 1) * CHUNK)
        for g in range(groups):
            cols = slice(g * gd, (g + 1) * gd)
            mixed = jnp.dot(ws_ref[g].astype(BF16), v[rows, cols], preferred_element_type=F32) + bst_ref[:, g:g + 1]
            u = _gelu(u_ref[rows, cols].astype(F32))
            o_ref[rows, cols] = (u * mixed * _silu(g_ref[rows, cols].astype(F32))).astype(o_ref.dtype)


def _chunk_mlp(z, vnw, ws, bst, *, u_blk, width):
    m = z.shape[0]
    groups = ws.shape[0]
    tr = _pick(m, WIDE_ROW_TILES)
    col = lambda b: pl.BlockSpec((tr, width), lambda i: (i, b))
    return pl.pallas_call(
        functools.partial(_mlp_kernel, groups=groups),
        grid=(m // tr,),
        in_specs=[col(u_blk), col(u_blk + 1), col(u_blk + 2),
                  pl.BlockSpec((1, width), lambda i: (0, 0)),
                  pl.BlockSpec((groups, CHUNK, CHUNK), lambda i: (0, 0, 0)),
                  pl.BlockSpec((CHUNK, groups), lambda i: (0, 0))],
        out_specs=pl.BlockSpec((tr, width), lambda i: (i, 0)),
        out_shape=jax.ShapeDtypeStruct((m, width), BF16),
        compiler_params=_params(("arbitrary",)),
        name="chunk_mlp",
    )(z, z, z, vnw, ws, bst)


def _out_proj_kernel(ret_ref, dif_ref, mlp_ref, wr_ref, wd_ref, wm_ref, x_ref, gate_ref, o_ref,
                     wrb_ref, wdb_ref, wmb_ref, *, seq, tm):
    @pl.when(pl.program_id(1) == 0)
    def _():
        wrb_ref[...] = wr_ref[...].astype(BF16)
        wdb_ref[...] = wd_ref[...].astype(BF16)
        wmb_ref[...] = wm_ref[...].astype(BF16)

    y = jnp.dot(ret_ref[...], wrb_ref[...], preferred_element_type=F32)
    y = y + jnp.dot(dif_ref[...], wdb_ref[...], preferred_element_type=F32)
    y = y + jnp.dot(mlp_ref[...], wmb_ref[...], preferred_element_type=F32)
    row = pl.program_id(1) * tm + lax.broadcasted_iota(jnp.int32, y.shape, 0)
    gate = jnp.where(row >= seq, gate_ref[1:2, :], gate_ref[0:1, :])
    o_ref[...] = x_ref[...] + gate * y


def _out_proj(ret, dif, mlp, w_out, layer, xx, mod, *, seq):
    m, d = xx.shape
    rw, dw, mw = ret.shape[1], dif.shape[1], mlp.shape[1]
    assert rw == dw and (rw + dw) % mw == 0
    tm = _pick(m, OUT_PROJ_TM)
    tn = _pick(d, OUT_PROJ_TN)
    gate_blk = 2 * d // tn
    return pl.pallas_call(
        functools.partial(_out_proj_kernel, seq=seq, tm=tm),
        grid=(d // tn, m // tm),
        in_specs=[
            pl.BlockSpec((tm, rw), lambda j, i: (i, 0)),
            pl.BlockSpec((tm, dw), lambda j, i: (i, 0)),
            pl.BlockSpec((tm, mw), lambda j, i: (i, 0)),
            pl.BlockSpec((None, rw, tn), lambda j, i: (layer, 0, j)),
            pl.BlockSpec((None, dw, tn), lambda j, i: (layer, 1, j)),
            pl.BlockSpec((None, mw, tn), lambda j, i: (layer, (rw + dw) // mw, j)),
            pl.BlockSpec((tm, tn), lambda j, i: (i, j)),
            pl.BlockSpec((8, tn), lambda j, i: (0, gate_blk + j)),
        ],
        out_specs=pl.BlockSpec((tm, tn), lambda j, i: (i, j)),
        out_shape=jax.ShapeDtypeStruct((m, d), F32),
        scratch_shapes=[pltpu.VMEM((rw, tn), BF16), pltpu.VMEM((dw, tn), BF16), pltpu.VMEM((mw, tn), BF16)],
        compiler_params=_params(("arbitrary", "arbitrary")),
        name="out_proj",
    )(ret, dif, mlp, w_out, w_out, w_out, xx, mod)


def _final_norm_kernel(x_ref, w_ref, o_ref):
    x = x_ref[...]
    o_ref[...] = x * lax.rsqrt(jnp.mean(x * x, axis=-1, keepdims=True) + EPS) * w_ref[...]


def _final_norm(xx, w, seq):
    d = xx.shape[1]
    tr = _pick(seq, WIDE_ROW_TILES)
    return pl.pallas_call(
        _final_norm_kernel,
        grid=(seq // tr,),
        in_specs=[pl.BlockSpec((tr, d), lambda i: (i, 0)), pl.BlockSpec((1, d), lambda i: (0, 0))],
        out_specs=pl.BlockSpec((tr, d), lambda i: (i, 0)),
        out_shape=jax.ShapeDtypeStruct((seq, d), F32),
        compiler_params=_params(("arbitrary",)),
        name="final_norm",
    )(xx, w)


def _rope_tables(seq, ctx, dim, repeats):
    rows = seq // GRID_W
    n_freq = dim // 4
    inv = ROPE_BASE ** (-jnp.arange(n_freq, dtype=F32) / n_freq)
    ang_r = jnp.arange(rows, dtype=F32)[:, None] * inv
    ang_c = jnp.arange(GRID_W, dtype=F32)[:, None] * inv

    def per_token(f):
        by_row = jnp.broadcast_to(f(ang_r)[:, None, :], (rows, GRID_W, n_freq))
        by_col = jnp.broadcast_to(f(ang_c)[None, :, :], (rows, GRID_W, n_freq))
        return jnp.concatenate([by_row, by_col], axis=-1).reshape(seq, 2 * n_freq)

    cos_t, sin_t = per_token(jnp.cos), per_token(jnp.sin)
    cos = jnp.concatenate([cos_t, cos_t], axis=-1)
    sin = jnp.concatenate([-sin_t, sin_t], axis=-1)
    cos = jnp.concatenate([jnp.tile(cos, (1, repeats)), jnp.ones((ctx, repeats * dim), F32)], axis=0)
    sin = jnp.concatenate([jnp.tile(sin, (1, repeats)), jnp.zeros((ctx, repeats * dim), F32)], axis=0)
    return cos, sin


def kernel(x, c, ctx, c_ctx, w_in, w_out, w_ada, b_ada, norm_w, ret_decay_fwd, ret_decay_bwd,
           lam_q1, lam_k1, lam_q2, lam_k2, dif_norm_w, mlp_vnorm_w, mlp_ws, mlp_bs, final_norm_w):
    batch, seq, d = x.shape
    assert batch == 1, "the kernels process one sequence"
    n_ctx = ctx.shape[1]
    depth = w_in.shape[0]
    heads = ret_decay_fwd.shape[1]
    ret_w = heads * HEAD_DIM
    dif_w = ret_w
    mlp_w = mlp_vnorm_w.shape[1]
    assert 2 * ret_w + mlp_w == d and w_in.shape[2] == 8 * ret_w + 3 * mlp_w

    rk_blk, rv_blk, dk_blk, dv_blk = 0, heads, 2 * heads, 3 * heads
    rq_blk, dq_blk, rg_blk, dg_blk = 4 * heads, 5 * heads, 6 * heads, 7 * heads
    mu_blk = 8 * ret_w // mlp_w

    xx = (x[0], ctx[0])
    cc = jnp.zeros((8, d), F32).at[0].set(c[0]).at[1].set(c_ctx)
    mods = _modulation(cc, w_ada, b_ada)
    cos_r, sin_r = _rope_tables(seq, n_ctx, HEAD_DIM, 1)
    cos_d, sin_d = _rope_tables(seq, n_ctx, HEAD_DIM // 2, 2)

    for l in range(depth):
        lambda_init = 0.8 - 0.6 * math.exp(-0.3 * l)
        h = _norm_mod(xx, norm_w[l][None, :], mods[l], seq)
        if l == 0:
            h, xx = h
        z = _in_proj(h, w_in, l, BF16)
        ret = _retention(z, ret_decay_fwd, ret_decay_bwd, cos_r, sin_r, layer=l, heads=heads, seq=seq,
                         q_blk=rq_blk, k_blk=rk_blk, v_blk=rv_blk, g_blk=rg_blk)
        lams = [a[l][None, :] for a in (lam_q1, lam_k1, lam_q2, lam_k2)]
        dif = _attention(z, lams, cos_d, sin_d, dif_norm_w[l][None, :], lambda_init=lambda_init,
                         heads=heads, seq=seq, q_blk=dq_blk, k_blk=dk_blk, v_blk=dv_blk, g_blk=dg_blk)
        mlp = _chunk_mlp(z, mlp_vnorm_w[l][None, :], mlp_ws[l], mlp_bs[l].T, u_blk=mu_blk, width=mlp_w)
        xx = _out_proj(ret, dif, mlp, w_out, l, xx, mods[l], seq=seq)

    return _final_norm(xx, final_norm_w[None, :], seq)[None]
```
